```python
import math
import jax
import jax.numpy as jnp
from jax import lax
import numpy as np


D_MODEL = 1024
BATCH = 8
SEQ = 8192
DEPTH = 4
DEC_BATCH = 2
DEC_SEQ = 8192
PAST_LEN = 128

GRID_W = 64
Q_BLOCK = 128
ATT_Q_HEADS = 8
ATT_KV_HEADS = 2
ATT_HEAD_DIM = 64
ATT_GROUP = ATT_Q_HEADS // ATT_KV_HEADS
ROPE_PAIRS = ATT_HEAD_DIM // 4
ROPE_THETA = 10000.0
NA_HEADS = 8
NA_HEAD_DIM = 64
NA_WIN_ROWS = 8
NA_WIN_COLS = 16
DN_HEADS = 8
DN_HEAD_DIM = 64
DN_CHUNK = 64
DN_CONV_W = 3
BRANCH_WIDTH = 512
N_BRANCHES = 3
FFN_HIDDEN = ((8 * D_MODEL + 3 * 256 - 1) // (3 * 256)) * 256
PLE_DIM = 256
LN_EPS = 1e-5
RMS_EPS = 1e-6
L2_EPS = 1e-6
DEEPNORM_ALPHA = (2 * DEPTH) ** 0.25
DEEPNORM_BETA = (8 * DEPTH) ** -0.25
IN_SIZES = (ATT_Q_HEADS * ATT_HEAD_DIM, ATT_KV_HEADS * ATT_HEAD_DIM, ATT_KV_HEADS * ATT_HEAD_DIM,
            NA_HEADS * NA_HEAD_DIM, NA_HEADS * NA_HEAD_DIM, NA_HEADS * NA_HEAD_DIM,
            DN_HEADS * DN_HEAD_DIM, DN_HEADS * DN_HEAD_DIM, DN_HEADS * DN_HEAD_DIM, DN_HEADS * DN_HEAD_DIM,
            DN_HEADS, DN_HEADS, DN_HEADS, DN_HEADS)
IN_COLS = sum(IN_SIZES)

kernel_name = 'hybrid_grid_encoder'


def layer_norm(x, g, b):
    xf = x.astype(jnp.float32)
    mu = jnp.mean(xf, axis=-1, keepdims=True)
    var = jnp.mean(jnp.square(xf - mu), axis=-1, keepdims=True)
    y = (xf - mu) * lax.rsqrt(var + LN_EPS) * g.astype(jnp.float32) + b.astype(jnp.float32)
    return y.astype(x.dtype)


def rms_norm(x, g):
    xf = x.astype(jnp.float32)
    y = xf * lax.rsqrt(jnp.mean(jnp.square(xf), axis=-1, keepdims=True) + RMS_EPS) * g.astype(jnp.float32)
    return y.astype(x.dtype)


def l2_normalize(x):
    xf = x.astype(jnp.float32)
    return (xf * lax.rsqrt(jnp.sum(jnp.square(xf), axis=-1, keepdims=True) + L2_EPS)).astype(x.dtype)


def axial_rope_tables(seq_len):
    t = jnp.arange(seq_len, dtype=jnp.int32)
    row = (t // GRID_W).astype(jnp.float32)
    col = (t % GRID_W).astype(jnp.float32)
    inv_freq = ROPE_THETA ** (-jnp.arange(ROPE_PAIRS, dtype=jnp.float32) / ROPE_PAIRS)
    ang_r = row[:, None] * inv_freq[None, :]
    ang_c = col[:, None] * inv_freq[None, :]
    return (jnp.cos(ang_r), jnp.sin(ang_r), jnp.cos(ang_c), jnp.sin(ang_c))


def apply_axial_rope(x, tables):
    cr, sr, cc, sc = (a[None, :, None, :] for a in tables)
    xf = x.astype(jnp.float32)
    P = ROPE_PAIRS
    r1, r2, c1, c2 = xf[..., :P], xf[..., P:2 * P], xf[..., 2 * P:3 * P], xf[..., 3 * P:]
    out = jnp.concatenate([r1 * cr - r2 * sr, r2 * cr + r1 * sr,
                           c1 * cc - c2 * sc, c2 * cc + c1 * sc], axis=-1)
    return out.astype(x.dtype)


def gqa_attention(q, k, v):
    b, s, _, hd = q.shape
    nblk = s // Q_BLOCK
    qb = jnp.moveaxis(q.reshape(b, nblk, Q_BLOCK, ATT_KV_HEADS, ATT_GROUP, hd), 1, 0)
    scale = hd ** -0.5

    def block(qi):
        sc = jnp.einsum('bqhgd,bkhd->bhgqk', qi, k, preferred_element_type=jnp.float32) * scale
        pr = jax.nn.softmax(sc, axis=-1)
        return jnp.einsum('bhgqk,bkhd->bqhgd', pr.astype(v.dtype), v)

    o = lax.map(block, qb)
    return jnp.moveaxis(o, 0, 1).reshape(b, s, ATT_Q_HEADS * hd)


def neighborhood_attention(q, k, v, rpb):
    b, s, h, hd = q.shape
    rows = s // GRID_W
    wr = min(NA_WIN_ROWS, rows)
    wc = NA_WIN_COLS
    t = jnp.arange(s, dtype=jnp.int32)
    r, c = t // GRID_W, t % GRID_W
    r0 = jnp.clip(r - wr // 2, 0, rows - wr)
    c0 = jnp.clip(c - wc // 2, 0, GRID_W - wc)
    j = jnp.arange(wr * wc, dtype=jnp.int32)
    kr = r0[:, None] + j[None, :] // wc
    kc = c0[:, None] + j[None, :] % wc
    idx = kr * GRID_W + kc
    dr = kr - r[:, None] + (NA_WIN_ROWS - 1)
    dc = kc - c[:, None] + (NA_WIN_COLS - 1)
    nblk = s // Q_BLOCK

    def blk(a):
        return a.reshape((nblk, Q_BLOCK) + a.shape[1:])

    qb = jnp.moveaxis(q.reshape(b, nblk, Q_BLOCK, h, hd), 1, 0)
    scale = hd ** -0.5
    bias_tab = rpb.astype(jnp.float32)

    def block(args):
        qi, ii, dri, dci = args
        ki = k[:, ii]
        vi = v[:, ii]
        sc = jnp.einsum('bqhd,bqwhd->bhqw', qi, ki, preferred_element_type=jnp.float32) * scale
        sc = sc + bias_tab[:, dri, dci][None]
        pr = jax.nn.softmax(sc, axis=-1)
        return jnp.einsum('bhqw,bqwhd->bqhd', pr.astype(vi.dtype), vi)

    o = lax.map(block, (qb, blk(idx), blk(dr), blk(dc)))
    return jnp.moveaxis(o, 0, 1).reshape(b, s, h * hd)


def centred_depthwise_conv(x, w):
    ch = x.shape[-1]
    return lax.conv_general_dilated(x, w[:, None, :].astype(x.dtype), window_strides=(1,),
                                    padding=[(DN_CONV_W // 2, DN_CONV_W // 2)],
                                    dimension_numbers=('NWC', 'WIO', 'NWC'),
                                    feature_group_count=ch)


def gated_delta_rule(q, k, v, g, beta):
    b, s, h, dk = q.shape
    dv = v.shape[-1]
    C = DN_CHUNK
    n = s // C

    def to_chunks(a):
        a = a.astype(jnp.float32).reshape((b, n, C, h) + a.shape[3:])
        return jnp.moveaxis(a, 3, 1)

    q, k, v, g, beta = (to_chunks(a) for a in (q, k, v, g, beta))
    gc = jnp.cumsum(g, axis=-1)
    incl = jnp.tril(jnp.ones((C, C), dtype=bool))
    strict = jnp.tril(jnp.ones((C, C), dtype=bool), -1)
    decay = jnp.exp(jnp.where(incl, gc[..., :, None] - gc[..., None, :], -jnp.inf))
    kb = k * beta[..., None]
    a_mat = jnp.where(strict, jnp.einsum('bhnid,bhnjd->bhnij', kb, k) * decay, 0.0)
    eye = jnp.eye(C, dtype=jnp.float32)
    t_mat = lax.linalg.triangular_solve(eye + a_mat, jnp.broadcast_to(eye, a_mat.shape),
                                        left_side=True, lower=True, unit_diagonal=True)
    u = jnp.einsum('bhnij,bhnjd->bhnid', t_mat, v * beta[..., None])
    w = jnp.einsum('bhnij,bhnjd->bhnid', t_mat, kb * jnp.exp(gc)[..., None])
    intra = jnp.where(incl, jnp.einsum('bhnid,bhnjd->bhnij', q, k) * decay, 0.0)
    q_dec = q * jnp.exp(gc)[..., None]
    k_dec = k * jnp.exp(gc[..., -1:] - gc)[..., None]
    g_tot = jnp.exp(gc[..., -1])

    def step(state, xs):
        u_i, w_i, intra_i, q_i, k_i, gt_i = xs
        v_new = u_i - jnp.einsum('bhcd,bhde->bhce', w_i, state)
        o_i = jnp.einsum('bhcd,bhde->bhce', q_i, state) + jnp.einsum('bhcj,bhje->bhce', intra_i, v_new)
        state = state * gt_i[..., None, None] + jnp.einsum('bhcd,bhce->bhde', k_i, v_new)
        return state, o_i

    xs = tuple(jnp.moveaxis(a, 2, 0) for a in (u, w, intra, q_dec, k_dec, g_tot))
    state0 = jnp.zeros((b, h, dk, dv), jnp.float32)
    _, o = lax.scan(step, state0, xs)
    return jnp.transpose(o, (1, 0, 3, 2, 4)).reshape(b, s, h, dv)


def deltanet_branch(cq, ck, cv, cz, a_f, a_b, b_f, b_b, conv_w, a_log, dt_bias, norm_g):
    bsz, s, _ = cq.shape
    hshape = (bsz, s, DN_HEADS, DN_HEAD_DIM)
    qkv = jax.nn.silu(centred_depthwise_conv(jnp.concatenate([cq, ck, cv], axis=-1), conv_w))
    q, k, v = jnp.split(qkv, 3, axis=-1)
    q = l2_normalize(q.reshape(hshape)) * (DN_HEAD_DIM ** -0.5)
    k = l2_normalize(k.reshape(hshape))
    v = v.reshape(hshape)

    def decay_and_beta(a, bb, d):
        g = -jnp.exp(a_log[d].astype(jnp.float32)) * jax.nn.softplus(a.astype(jnp.float32) + dt_bias[d].astype(jnp.float32))
        return g, jax.nn.sigmoid(bb.astype(jnp.float32))

    g_f, beta_f = decay_and_beta(a_f, b_f, 0)
    g_b, beta_b = decay_and_beta(a_b, b_b, 1)
    o_fwd = gated_delta_rule(q, k, v, g_f, beta_f)

    def rev(a):
        return jnp.flip(a, axis=1)

    o_bwd = rev(gated_delta_rule(rev(q), rev(k), rev(v), rev(g_b), rev(beta_b)))
    o = rms_norm(o_fwd + o_bwd, norm_g) * jax.nn.silu(cz.reshape(hshape).astype(jnp.float32))
    return o.reshape(bsz, s, DN_HEADS * DN_HEAD_DIM).astype(cq.dtype)


def encoder_layer(x, p_i, i, prm):
    bsz, s, _ = x.shape
    split_points = np.cumsum(IN_SIZES)[:-1].tolist()
    (aq, ak, av, nq, nk, nv, cq, ck, cv, cz,
     ca_f, ca_b, cb_f, cb_b) = jnp.split(x @ prm['w_in'][i], split_points, axis=-1)
    tables = axial_rope_tables(s)
    qa = apply_axial_rope(rms_norm(aq.reshape(bsz, s, ATT_Q_HEADS, ATT_HEAD_DIM), prm['att_q_norm_g'][i]), tables)
    ka = apply_axial_rope(rms_norm(ak.reshape(bsz, s, ATT_KV_HEADS, ATT_HEAD_DIM), prm['att_k_norm_g'][i]), tables)
    va = av.reshape(bsz, s, ATT_KV_HEADS, ATT_HEAD_DIM)
    br_a = gqa_attention(qa, ka, va)
    na_shape = (bsz, s, NA_HEADS, NA_HEAD_DIM)
    br_b = neighborhood_attention(nq.reshape(na_shape), nk.reshape(na_shape), nv.reshape(na_shape), prm['na_rpb'][i])
    br_c = deltanet_branch(cq, ck, cv, cz, ca_f, ca_b, cb_f, cb_b, prm['dn_conv_w'][i],
                           prm['dn_a_log'][i], prm['dn_dt_bias'][i], prm['dn_norm_g'][i])
    gates = jax.nn.sigmoid(x @ prm['w_gate'][i] + prm['b_gate'][i])
    g_a, g_b, g_c = jnp.split(gates, N_BRANCHES, axis=-1)
    w_br = prm['w_branch'][i]
    merged = g_a * (br_a @ w_br[0]) + g_b * (br_b @ w_br[1]) + g_c * (br_c @ w_br[2])
    x = layer_norm(DEEPNORM_ALPHA * x + merged @ prm['w_out'][i], prm['ln1_g'][i], prm['ln1_b'][i])
    gate, up = jnp.split(x @ prm['w_ffn_in'][i], 2, axis=-1)
    ffn = (jax.nn.silu(gate) * up) @ prm['w_ffn_out'][i]
    ple = jax.nn.sigmoid(x @ prm['w_ple_gate'][i] + prm['b_ple_gate'][i]) * (p_i @ prm['w_ple_proj'][i])
    return layer_norm(DEEPNORM_ALPHA * x + ffn + ple, prm['ln2_g'][i], prm['ln2_b'][i])


def run_trunk(x, p, emb_ln_g, emb_ln_b, prm):
    x = layer_norm(x, emb_ln_g, emb_ln_b)
    for i in range(DEPTH):
        x = encoder_layer(x, p[i], i, prm)
    return x


def setup_inputs(seed: int = 0) -> dict:
    key = jax.random.key(seed)
    ks = jax.random.split(key, 32)
    f32 = jnp.float32

    def nrm(k, shape, scale):
        return jax.random.normal(k, shape, f32) * scale

    inv_d = D_MODEL ** -0.5
    dt = jnp.exp(jax.random.uniform(ks[12], (DEPTH, 2, DN_HEADS), f32, math.log(1e-3), math.log(1e-1)))
    return {
        'x_prompt': nrm(ks[0], (BATCH, SEQ, D_MODEL), 1.0),
        'x_sample': nrm(ks[1], (DEC_BATCH, DEC_SEQ, D_MODEL), 1.0),
        'p_prompt': nrm(ks[2], (DEPTH, BATCH, SEQ, PLE_DIM), 1.0),
        'p_sample': nrm(ks[3], (DEPTH, DEC_BATCH, DEC_SEQ, PLE_DIM), 1.0),
        'emb_ln_g': 1.0 + nrm(ks[4], (D_MODEL,), 0.02),
        'emb_ln_b': nrm(ks[5], (D_MODEL,), 0.02),
        'w_in': nrm(ks[6], (DEPTH, D_MODEL, IN_COLS), inv_d),
        'att_q_norm_g': 1.0 + nrm(ks[7], (DEPTH, ATT_HEAD_DIM), 0.02),
        'att_k_norm_g': 1.0 + nrm(ks[8], (DEPTH, ATT_HEAD_DIM), 0.02),
        'na_rpb': nrm(ks[9], (DEPTH, NA_HEADS, 2 * NA_WIN_ROWS - 1, 2 * NA_WIN_COLS - 1), 0.02),
        'dn_conv_w': nrm(ks[10], (DEPTH, DN_CONV_W, 3 * DN_HEADS * DN_HEAD_DIM), DN_CONV_W ** -0.5),
        'dn_a_log': jnp.log(jax.random.uniform(ks[11], (DEPTH, 2, DN_HEADS), f32, 1.0, 16.0)),
        'dn_dt_bias': dt + jnp.log(-jnp.expm1(-dt)),
        'dn_norm_g': 1.0 + nrm(ks[13], (DEPTH, DN_HEAD_DIM), 0.02),
        'w_gate': nrm(ks[14], (DEPTH, D_MODEL, N_BRANCHES * D_MODEL), inv_d),
        'b_gate': nrm(ks[15], (DEPTH, N_BRANCHES * D_MODEL), 0.02),
        'w_branch': nrm(ks[16], (DEPTH, N_BRANCHES, BRANCH_WIDTH, D_MODEL), BRANCH_WIDTH ** -0.5),
        'w_out': nrm(ks[17], (DEPTH, D_MODEL, D_MODEL), inv_d * DEEPNORM_BETA),
        'ln1_g': 1.0 + nrm(ks[18], (DEPTH, D_MODEL), 0.02),
        'ln1_b': nrm(ks[19], (DEPTH, D_MODEL), 0.02),
        'w_ffn_in': nrm(ks[20], (DEPTH, D_MODEL, 2 * FFN_HIDDEN), inv_d),
        'w_ffn_out': nrm(ks[21], (DEPTH, FFN_HIDDEN, D_MODEL), FFN_HIDDEN ** -0.5 * DEEPNORM_BETA),
        'w_ple_gate': nrm(ks[22], (DEPTH, D_MODEL, D_MODEL), inv_d),
        'b_ple_gate': nrm(ks[23], (DEPTH, D_MODEL), 0.02),
        'w_ple_proj': nrm(ks[24], (DEPTH, PLE_DIM, D_MODEL), PLE_DIM ** -0.5),
        'ln2_g': 1.0 + nrm(ks[25], (DEPTH, D_MODEL), 0.02),
        'ln2_b': nrm(ks[26], (DEPTH, D_MODEL), 0.02),
    }


def reference(x_prompt, x_sample, p_prompt, p_sample, emb_ln_g, emb_ln_b, w_in,
              att_q_norm_g, att_k_norm_g, na_rpb, dn_conv_w, dn_a_log, dn_dt_bias, dn_norm_g,
              w_gate, b_gate, w_branch, w_out, ln1_g, ln1_b, w_ffn_in, w_ffn_out,
              w_ple_gate, b_ple_gate, w_ple_proj, ln2_g, ln2_b):
    prm = {
        'w_in': w_in, 'att_q_norm_g': att_q_norm_g, 'att_k_norm_g': att_k_norm_g,
        'na_rpb': na_rpb, 'dn_conv_w': dn_conv_w, 'dn_a_log': dn_a_log,
        'dn_dt_bias': dn_dt_bias, 'dn_norm_g': dn_norm_g, 'w_gate': w_gate, 'b_gate': b_gate,
        'w_branch': w_branch, 'w_out': w_out, 'ln1_g': ln1_g, 'ln1_b': ln1_b,
        'w_ffn_in': w_ffn_in, 'w_ffn_out': w_ffn_out, 'w_ple_gate': w_ple_gate,
        'b_ple_gate': b_ple_gate, 'w_ple_proj': w_ple_proj, 'ln2_g': ln2_g, 'ln2_b': ln2_b,
    }
    y_prompt = run_trunk(x_prompt, p_prompt, emb_ln_g, emb_ln_b, prm)
    y_sample = run_trunk(x_sample, p_sample, emb_ln_g, emb_ln_b, prm)
    return (y_prompt, y_sample)
```

```python
import functools

import numpy as np
import jax
import jax.numpy as jnp
from jax import lax
from jax.experimental import pallas as pl
from jax.experimental.pallas import tpu as pltpu

F32 = jnp.float32
BF16 = jnp.bfloat16

D_MODEL = 1024
DEPTH = 4
GRID_W = 64
HEAD_DIM = 64
ATT_Q_HEADS = 8
ATT_KV_HEADS = 2
ROPE_PAIRS = HEAD_DIM // 4
ROPE_THETA = 10000.0
NA_HEADS = 8
NA_WIN_ROWS = 8
NA_WIN_COLS = 16
DN_HEADS = 8
DN_CHUNK = 64
BRANCH_WIDTH = 512
FFN_HIDDEN = 2816
PLE_DIM = 256
LN_EPS = 1e-5
RMS_EPS = 1e-6
L2_EPS = 1e-6
DEEPNORM_ALPHA = (2 * DEPTH) ** 0.25

LANES = 128
HALO = 16
VMEM_LIMIT = 56 * 1024 * 1024
NEG = -1e30

_OFF = np.cumsum([0, 512, 128, 128, 512, 512, 512, 512, 512, 512, 512, 8, 8, 8, 8])


def _cparams(sem):
    return pltpu.CompilerParams(dimension_semantics=sem, vmem_limit_bytes=VMEM_LIMIT)


def _const_spec(shape):
    nd = len(shape)
    return pl.BlockSpec(shape, lambda *a: (0,) * nd, pipeline_mode=pl.Buffered(1))


def _bdot(a, b):
    return jnp.dot(a.astype(BF16), b.astype(BF16), preferred_element_type=F32)


def _split(x, n):
    out = []
    r = x
    for _ in range(n):
        p = r.astype(BF16)
        out.append(p)
        r = r - p.astype(F32)
    return out


def _dot_exact_rhs(a, b01, n=3):
    acc = None
    for p in _split(a, n):
        t = jnp.dot(p, b01, preferred_element_type=F32)
        acc = t if acc is None else acc + t
    return acc


def _dot_exact_lhs(a01, b, n=3):
    acc = None
    for p in _split(b, n):
        t = jnp.dot(a01, p, preferred_element_type=F32)
        acc = t if acc is None else acc + t
    return acc


def _seg_ones(n, seg):
    r = lax.broadcasted_iota(jnp.int32, (n, n), 0) // seg
    c = lax.broadcasted_iota(jnp.int32, (n, n), 1) // seg
    return jnp.where(r == c, 1.0, 0.0).astype(BF16)


def _seg_sum(x, ones):
    return _dot_exact_rhs(x, ones, 2)


def _layer_norm(y, g, b):
    mu = jnp.mean(y, axis=-1, keepdims=True)
    yc = y - mu
    var = jnp.mean(yc * yc, axis=-1, keepdims=True)
    return yc * lax.rsqrt(var + LN_EPS) * g + b


def _silu(x):
    return x * jax.nn.sigmoid(x)


def _ln_kernel(x_ref, g_ref, b_ref, o_ref):
    o_ref[...] = _layer_norm(x_ref[...], g_ref[...], b_ref[...])


def _ln_call(x, g, b, tm=1024):
    t, d = x.shape
    return pl.pallas_call(
        _ln_kernel,
        grid=(t // tm,),
        in_specs=[pl.BlockSpec((tm, d), lambda i: (i, 0)), _const_spec((1, d)), _const_spec((1, d))],
        out_specs=pl.BlockSpec((tm, d), lambda i: (i, 0)),
        out_shape=jax.ShapeDtypeStruct((t, d), F32),
        compiler_params=_cparams(("parallel",)),
        name="emb_ln",
    )(x, g.reshape(1, d), b.reshape(1, d))


def _proj_kernel(xp_ref, x_ref, xn_ref, cos_ref, sin_ref, wa_ref, wn_ref, wc_ref, wz_ref, wab_ref,
                 qg_ref, kg_ref, cw_ref, gp_ref,
                 qa_ref, kt_ref, va_ref, nq_ref, nkt_ref, nv_ref, dq_ref, dk_ref, dv_ref, cz_ref, gn_ref,
                 xs_ref, c_ref, *, tm, nts):
    i = pl.program_id(0)
    first = (i % nts) == 0
    last = (i % nts) == nts - 1
    xs_ref[0:HALO, :] = jnp.where(first, 0.0, xp_ref[...]).astype(BF16)
    xs_ref[HALO:HALO + tm, :] = x_ref[...].astype(BF16)
    xs_ref[HALO + tm:, :] = jnp.where(last, 0.0, xn_ref[...]).astype(BF16)
    xm = xs_ref[HALO:HALO + tm, :]

    ones = _seg_ones(LANES, HEAD_DIM)
    cos = cos_ref[...]
    sin = sin_ref[...]
    lane = lax.broadcasted_iota(jnp.int32, (tm, LANES), 1)
    first_half = (lane % (2 * ROPE_PAIRS)) < ROPE_PAIRS

    def norm_rope(xq, g):
        ss = _seg_sum(xq * xq, ones)
        xn = xq * lax.rsqrt(ss * (1.0 / HEAD_DIM) + RMS_EPS) * g
        partner = jnp.where(first_half, pltpu.roll(xn, LANES - ROPE_PAIRS, 1), pltpu.roll(xn, ROPE_PAIRS, 1))
        return xn * cos + partner * sin

    ya = jnp.dot(xm, wa_ref[...], preferred_element_type=F32)
    scale = HEAD_DIM ** -0.5
    for j in range(4):
        sl = slice(LANES * j, LANES * (j + 1))
        qa_ref[:, sl] = (norm_rope(ya[:, sl], qg_ref[...]) * scale).astype(BF16)
    for j in range(2):
        sl = slice(512 + LANES * j, 512 + LANES * (j + 1))
        kt_ref[0, LANES * j:LANES * (j + 1), :] = norm_rope(ya[:, sl], kg_ref[...]).T.astype(BF16)
    va_ref[...] = ya[:, 768:1024].astype(BF16)

    yn = jnp.dot(xm, wn_ref[...], preferred_element_type=F32)
    nq_ref[...] = (yn[:, 0:512] * scale).astype(BF16)
    nkt_ref[0] = yn[:, 512:1024].T.astype(BF16)
    nv_ref[...] = yn[:, 1024:1536].astype(BF16)

    c_ref[...] = jnp.dot(xs_ref[...], wc_ref[...], preferred_element_type=F32)
    for j in range(12):
        sl = slice(LANES * j, LANES * (j + 1))
        y = (c_ref[HALO - 1:HALO - 1 + tm, sl] * cw_ref[0:1, sl]
             + c_ref[HALO:HALO + tm, sl] * cw_ref[1:2, sl]
             + c_ref[HALO + 1:HALO + 1 + tm, sl] * cw_ref[2:3, sl])
        y = _silu(y)
        if j < 8:
            y = y * lax.rsqrt(_seg_sum(y * y, ones) + L2_EPS)
        if j < 4:
            y = y * scale
        osl = slice(LANES * (j % 4), LANES * (j % 4 + 1))
        (dq_ref, dk_ref, dv_ref)[j // 4][:, osl] = y
    cz_ref[...] = jnp.dot(xm, wz_ref[...], preferred_element_type=F32)

    raw = jnp.dot(xm, wab_ref[...], preferred_element_type=F32)
    a_log = gp_ref[0:1, :]
    dt_b = gp_ref[1:2, :]
    is_g = gp_ref[2:3, :] > 0.5
    z = raw + dt_b
    softplus = jnp.maximum(z, 0.0) + jnp.log(1.0 + jnp.exp(-jnp.abs(z)))
    gn_ref[...] = jnp.where(is_g, -jnp.exp(a_log) * softplus, jax.nn.sigmoid(raw))


def _proj_call(x, bsz, seq, cos_t, sin_t, wa, wn, wc, wz, wab, qg, kg, cw, gp, tm=512):
    t = x.shape[0]
    nts = seq // tm
    nh = t // HALO
    row = lambda i: (i, 0)
    seqrow = lambda i: (i % nts, 0)
    in_specs = [
        pl.BlockSpec((HALO, D_MODEL), lambda i: (jnp.maximum(i * (tm // HALO) - 1, 0), 0)),
        pl.BlockSpec((tm, D_MODEL), row),
        pl.BlockSpec((HALO, D_MODEL), lambda i: (jnp.minimum((i + 1) * (tm // HALO), nh - 1), 0)),
        pl.BlockSpec((tm, LANES), seqrow),
        pl.BlockSpec((tm, LANES), seqrow),
        _const_spec(wa.shape), _const_spec(wn.shape), _const_spec(wc.shape), _const_spec(wz.shape),
        _const_spec(wab.shape), _const_spec(qg.shape), _const_spec(kg.shape), _const_spec(cw.shape),
        _const_spec(gp.shape),
    ]
    tposed = lambda i: (i // nts, 0, i % nts)
    out_specs = [
        pl.BlockSpec((tm, 512), row),
        pl.BlockSpec((1, 256, tm), tposed),
        pl.BlockSpec((tm, 256), row),
        pl.BlockSpec((tm, 512), row),
        pl.BlockSpec((1, 512, tm), tposed),
        pl.BlockSpec((tm, 512), row),
        pl.BlockSpec((tm, 512), row),
        pl.BlockSpec((tm, 512), row),
        pl.BlockSpec((tm, 512), row),
        pl.BlockSpec((tm, 512), row),
        pl.BlockSpec((tm, LANES), row),
    ]
    out_shape = [
        jax.ShapeDtypeStruct((t, 512), BF16),
        jax.ShapeDtypeStruct((bsz, 256, seq), BF16),
        jax.ShapeDtypeStruct((t, 256), BF16),
        jax.ShapeDtypeStruct((t, 512), BF16),
        jax.ShapeDtypeStruct((bsz, 512, seq), BF16),
        jax.ShapeDtypeStruct((t, 512), BF16),
        jax.ShapeDtypeStruct((t, 512), F32),
        jax.ShapeDtypeStruct((t, 512), F32),
        jax.ShapeDtypeStruct((t, 512), F32),
        jax.ShapeDtypeStruct((t, 512), F32),
        jax.ShapeDtypeStruct((t, LANES), F32),
    ]
    return pl.pallas_call(
        functools.partial(_proj_kernel, tm=tm, nts=nts),
        grid=(t // tm,),
        in_specs=in_specs,
        out_specs=out_specs,
        out_shape=out_shape,
        scratch_shapes=[pltpu.VMEM((tm + 2 * HALO, D_MODEL), BF16), pltpu.VMEM((tm + 2 * HALO, 1536), F32)],
        compiler_params=_cparams(("parallel",)),
        name="proj",
    )(x, x, x, cos_t, sin_t, wa, wn, wc, wz, wab, qg, kg, cw, gp)


def _gqa_kernel(q_ref, kt_ref, v_ref, o_ref, *, tq):
    lane = lax.broadcasted_iota(jnp.int32, (tq, LANES), 1)
    lo = lane < HEAD_DIM
    for h in range(ATT_KV_HEADS):
        parts = []
        for pp in range(2):
            qp = q_ref[0, :, 256 * h + LANES * pp:256 * h + LANES * (pp + 1)]
            zero = jnp.zeros_like(qp)
            parts.append(jnp.where(lo, qp, zero))
            parts.append(jnp.where(lo, zero, qp))
        q4 = jnp.concatenate(parts, axis=0)
        s = jnp.dot(q4, kt_ref[0, LANES * h:LANES * (h + 1), :], preferred_element_type=F32)
        m = jnp.max(s, axis=-1, keepdims=True)
        p = jnp.exp(s - m)
        l = jnp.sum(p, axis=-1, keepdims=True)
        pv = jnp.dot(p.astype(BF16), v_ref[0, :, LANES * h:LANES * (h + 1)], preferred_element_type=F32)
        pv = pv / l
        for pp in range(2):
            even = pv[(2 * pp) * tq:(2 * pp + 1) * tq]
            odd = pv[(2 * pp + 1) * tq:(2 * pp + 2) * tq]
            o_ref[0, :, 256 * h + LANES * pp:256 * h + LANES * (pp + 1)] = jnp.where(lo, even, odd).astype(BF16)


def _gqa_call(qa, kt, va, tq=64):
    bsz, seq, _ = qa.shape
    return pl.pallas_call(
        functools.partial(_gqa_kernel, tq=tq),
        grid=(bsz, seq // tq),
        in_specs=[
            pl.BlockSpec((1, tq, 512), lambda b, i: (b, i, 0)),
            pl.BlockSpec((1, 256, seq), lambda b, i: (b, 0, 0)),
            pl.BlockSpec((1, seq, 256), lambda b, i: (b, 0, 0)),
        ],
        out_specs=pl.BlockSpec((1, tq, 512), lambda b, i: (b, i, 0)),
        out_shape=jax.ShapeDtypeStruct((bsz, seq, 512), BF16),
        compiler_params=_cparams(("parallel", "arbitrary")),
        name="gqa",
    )(qa, kt, va)


NA_QBLK = 128


def _na_geometry(seq):
    rows = seq // GRID_W
    wr = min(NA_WIN_ROWS, rows)
    wc = NA_WIN_COLS
    nblk = seq // NA_QBLK
    nkb = min(5, nblk)
    starts = np.clip(np.arange(nblk) - 2, 0, nblk - nkb)
    variants, var_of = [], []
    for i in range(nblk):
        t = i * NA_QBLK + np.arange(NA_QBLK)
        r, c = t // GRID_W, t % GRID_W
        r0 = np.clip(r - wr // 2, 0, rows - wr)
        c0 = np.clip(c - wc // 2, 0, GRID_W - wc)
        kt = starts[i] * NA_QBLK + np.arange(nkb * NA_QBLK)
        kr, kc = kt // GRID_W, kt % GRID_W
        valid = ((kr[None] >= r0[:, None]) & (kr[None] < r0[:, None] + wr)
                 & (kc[None] >= c0[:, None]) & (kc[None] < c0[:, None] + wc))
        assert (valid.sum(1) == wr * wc).all()
        dr = np.where(valid, kr[None] - r[:, None] + (NA_WIN_ROWS - 1), 0)
        dc = np.where(valid, kc[None] - c[:, None] + (NA_WIN_COLS - 1), 0)
        key = (valid.tobytes(), dr.tobytes(), dc.tobytes())
        for vi, (k2, _) in enumerate(variants):
            if k2 == key:
                var_of.append(vi)
                break
        else:
            var_of.append(len(variants))
            variants.append((key, (valid, dr, dc)))
    pats = [v[1] for v in variants]
    return nkb, starts.astype(np.int32), np.asarray(var_of, np.int32), pats


def _na_bias_table(rpb, pats):
    tabs = []
    for valid, dr, dc in pats:
        b = rpb[:, dr, dc]
        tabs.append(jnp.where(valid[None], b, NEG))
    tab = jnp.stack(tabs)
    nv, h, q, k = tab.shape
    return tab.reshape(nv, h // 2, 2 * q, k)


def _na_kernel(var_ref, kb_ref, q_ref, *refs, nkb):
    kt_refs = refs[:nkb]
    v_refs = refs[nkb:2 * nkb]
    bias_ref = refs[2 * nkb]
    o_ref = refs[2 * nkb + 1]
    lane = lax.broadcasted_iota(jnp.int32, (NA_QBLK, LANES), 1)
    lo = lane < HEAD_DIM
    for pp in range(NA_HEADS // 2):
        sl = slice(LANES * pp, LANES * (pp + 1))
        qp = q_ref[0, :, sl]
        zero = jnp.zeros_like(qp)
        q2 = jnp.concatenate([jnp.where(lo, qp, zero), jnp.where(lo, zero, qp)], axis=0)
        ktp = jnp.concatenate([r[0, sl, :] for r in kt_refs], axis=1)
        vp = jnp.concatenate([r[0, :, sl] for r in v_refs], axis=0)
        s = jnp.dot(q2, ktp, preferred_element_type=F32) + bias_ref[0, pp]
        m = jnp.max(s, axis=-1, keepdims=True)
        p = jnp.exp(s - m)
        l = jnp.sum(p, axis=-1, keepdims=True)
        pv = jnp.dot(p.astype(BF16), vp, preferred_element_type=F32) / l
        o_ref[0, :, sl] = jnp.where(lo, pv[:NA_QBLK], pv[NA_QBLK:]).astype(BF16)


def _na_call(nq, nkt, nv, rpb):
    bsz, seq, _ = nq.shape
    nkb, starts, var_of, pats = _na_geometry(seq)
    bias = _na_bias_table(rpb.astype(F32), pats)
    nblk = seq // NA_QBLK
    in_specs = [pl.BlockSpec((1, NA_QBLK, 512), lambda b, i, var, kb: (b, i, 0))]
    for j in range(nkb):
        in_specs.append(pl.BlockSpec((1, 512, NA_QBLK), lambda b, i, var, kb, j=j: (b, 0, kb[i] + j)))
    for j in range(nkb):
        in_specs.append(pl.BlockSpec((1, NA_QBLK, 512), lambda b, i, var, kb, j=j: (b, kb[i] + j, 0)))
    in_specs.append(pl.BlockSpec((1,) + bias.shape[1:], lambda b, i, var, kb: (var[i], 0, 0, 0)))
    grid_spec = pltpu.PrefetchScalarGridSpec(
        num_scalar_prefetch=2,
        grid=(bsz, nblk),
        in_specs=in_specs,
        out_specs=pl.BlockSpec((1, NA_QBLK, 512), lambda b, i, var, kb: (b, i, 0)),
    )
    return pl.pallas_call(
        functools.partial(_na_kernel, nkb=nkb),
        grid_spec=grid_spec,
        out_shape=jax.ShapeDtypeStruct((bsz, seq, 512), BF16),
        compiler_params=_cparams(("parallel", "arbitrary")),
        name="na",
    )(jnp.asarray(var_of), jnp.asarray(starts), nq, *([nkt] * nkb), *([nv] * nkb), bias)


DN_TC = 512
DN_INV_PASSES = 3


def _mm_split(a, b, passes):
    if passes == 1:
        return _bdot(a, b)
    a_hi, a_lo = _split(a, 2)
    b_hi, b_lo = _split(b, 2)
    return (jnp.dot(a_hi, b_hi, preferred_element_type=F32)
            + jnp.dot(a_lo, b_hi, preferred_element_type=F32)
            + jnp.dot(a_hi, b_lo, preferred_element_type=F32))


def _dn_kernel(q_ref, k_ref, v_ref, gn_ref, gr_ref, o_ref, s_ref, gc_ref, g_ref, b_ref, gcr_ref):
    hp = pl.program_id(1)
    d = pl.program_id(2)
    c = pl.program_id(3)
    C = DN_CHUNK
    nch = DN_TC // C
    sgn = 1 - 2 * d

    @pl.when(c == 0)
    def _():
        s_ref[...] = jnp.zeros_like(s_ref)

    col = lax.broadcasted_iota(jnp.int32, (LANES, 2 * LANES), 0)
    ln2 = lax.broadcasted_iota(jnp.int32, (LANES, 2 * LANES), 1)
    want = (ln2 // LANES) * 16 + d * 8 + 2 * hp + (ln2 % LANES) // HEAD_DIM
    onehot = jnp.where(col == want, 1.0, 0.0).astype(BF16)
    gb = _dot_exact_rhs(gn_ref[0], onehot)
    g_col = gb[:, :LANES]
    g_ref[...] = g_col
    b_ref[...] = gb[:, LANES:]
    r = lax.broadcasted_iota(jnp.int32, (DN_TC, DN_TC), 0)
    cc = lax.broadcasted_iota(jnp.int32, (DN_TC, DN_TC), 1)
    tri = jnp.where((r // C == cc // C) & ((r - cc) * sgn >= 0), 1.0, 0.0).astype(BF16)
    gc_ref[...] = _dot_exact_lhs(tri, g_col)
    r2 = lax.broadcasted_iota(jnp.int32, (LANES, LANES), 0)
    c2 = lax.broadcasted_iota(jnp.int32, (LANES, LANES), 1)
    tri2 = jnp.where((r2 // C == c2 // C) & ((c2 - r2) * sgn >= 0), 1.0, 0.0).astype(BF16)
    gcr_ref[...] = _dot_exact_rhs(gr_ref[0, 0, 0], tri2)

    row = lax.broadcasted_iota(jnp.int32, (C, LANES), 0)
    lane = lax.broadcasted_iota(jnp.int32, (C, LANES), 1)
    lo = lane < HEAD_DIM
    jj = lane % HEAD_DIM
    earlier_eq = (row - jj) * sgn >= 0
    earlier = (row - jj) * sgn > 0
    eye2 = jnp.where(row == jj, 1.0, 0.0)
    r3 = lax.broadcasted_iota(jnp.int32, (LANES, LANES), 0) // HEAD_DIM
    c3 = lax.broadcasted_iota(jnp.int32, (LANES, LANES), 1) // HEAD_DIM
    same_head = r3 == c3

    def bd(x):
        return jnp.concatenate([jnp.where(lo, x, 0.0), jnp.where(lo, 0.0, x)], axis=0)

    def body(j, state):
        jc = j + d * (nch - 1 - 2 * j)
        rows = pl.ds(pl.multiple_of(jc * C, C), C)
        q = q_ref[0, rows, :]
        k = k_ref[0, rows, :]
        v = v_ref[0, rows, :]
        gc = gc_ref[rows, :]
        beta = b_ref[rows, :]
        tot = jnp.sum(g_ref[rows, :], axis=0, keepdims=True)
        gcr = gcr_ref[pl.ds(jc, 1), :]
        decay = jnp.where(earlier_eq, jnp.exp(jnp.minimum(gc - gcr, 0.0)), 0.0)
        kb = k * beta
        kx = bd(k).astype(BF16)
        ai = lax.dot_general(jnp.concatenate([kb, q], axis=0).astype(BF16), kx,
                             (((1,), (1,)), ((), ())), preferred_element_type=F32)
        a = jnp.where(earlier, ai[:C] * decay, 0.0)
        intra = ai[C:] * decay
        x = -a
        t = eye2 + x
        p = _mm_split(x, bd(x), DN_INV_PASSES)
        for lvl in range(5):
            pbd = bd(p)
            if lvl < 4:
                rr = _mm_split(jnp.concatenate([p, t], axis=0), pbd, DN_INV_PASSES)
                p = rr[:C]
                t = t + rr[C:]
            else:
                t = t + _mm_split(t, pbd, DN_INV_PASSES)
        egc = jnp.exp(gc)
        rhs = jnp.concatenate([bd(v * beta), bd(kb * egc)], axis=1)
        uw = _bdot(t, rhs)
        u = uw[:, :LANES]
        w = uw[:, LANES:]
        qd = q * egc
        kd = k * jnp.exp(tot - gc)
        wq = _bdot(jnp.concatenate([w, qd], axis=0), state)
        v_new = u - wq[:C]
        o = wq[C:] + _bdot(intra, bd(v_new))
        o_ref[0, 0, rows, :] = o
        upd = _bdot(kd.T, v_new)
        return state * jnp.exp(tot) + jnp.where(same_head, upd, 0.0)

    s_ref[...] = lax.fori_loop(0, nch, body, s_ref[...])


def _dn_call(dq, dk, dv, gn, gr):
    bsz, seq, _ = dq.shape
    nsc = seq // DN_TC
    nch = DN_TC // DN_CHUNK
    tmap = lambda c, d: c + d * (nsc - 1 - 2 * c)
    qkv_spec = pl.BlockSpec((1, DN_TC, LANES), lambda b, hp, d, c: (b, tmap(c, d), hp))
    return pl.pallas_call(
        _dn_kernel,
        grid=(bsz, DN_HEADS // 2, 2, nsc),
        in_specs=[
            qkv_spec, qkv_spec, qkv_spec,
            pl.BlockSpec((1, DN_TC, LANES), lambda b, hp, d, c: (b, tmap(c, d), 0)),
            pl.BlockSpec((1, 1, 1, nch, LANES), lambda b, hp, d, c: (d, hp, b, tmap(c, d), 0)),
        ],
        out_specs=pl.BlockSpec((1, 1, DN_TC, LANES), lambda b, hp, d, c: (d, b, tmap(c, d), hp)),
        out_shape=jax.ShapeDtypeStruct((2, bsz, seq, 512), F32),
        scratch_shapes=[
            pltpu.VMEM((LANES, LANES), F32),
            pltpu.VMEM((DN_TC, LANES), F32),
            pltpu.VMEM((DN_TC, LANES), F32),
            pltpu.VMEM((DN_TC, LANES), F32),
            pltpu.VMEM((nch, LANES), F32),
        ],
        compiler_params=_cparams(("parallel", "parallel", "arbitrary", "arbitrary")),
        name="deltanet",
    )(dq, dk, dv, gn, gr)


def _dn_rows(gn, bsz, seq):
    g = gn[:, :32].reshape(bsz, seq // DN_CHUNK, DN_CHUNK, 2, 2, DN_HEADS // 2, 2)
    g = g[:, :, :, 0]
    g = jnp.transpose(g, (3, 4, 0, 1, 5, 2))
    return g.reshape(2, DN_HEADS // 2, bsz, seq // DN_CHUNK, LANES)


def _merge_kernel(x_ref, bra_ref, brb_ref, of_ref, ob_ref, cz_ref, wg_ref, bg_ref, wb_ref, wo_ref,
                  ng_ref, g1_ref, b1_ref, o_ref):
    x = x_ref[...]
    xb = x.astype(BF16)
    ones = _seg_ones(LANES, HEAD_DIM)
    parts = []
    for j in range(4):
        sl = slice(LANES * j, LANES * (j + 1))
        o = of_ref[0, :, sl] + ob_ref[0, :, sl]
        ss = _seg_sum(o * o, ones)
        parts.append((o * lax.rsqrt(ss * (1.0 / HEAD_DIM) + RMS_EPS) * ng_ref[...] * _silu(cz_ref[:, sl])).astype(BF16))
    brc = jnp.concatenate(parts, axis=1)
    merged = None
    for n, br in enumerate((bra_ref[...], brb_ref[...], brc)):
        sl = slice(D_MODEL * n, D_MODEL * (n + 1))
        gate = jax.nn.sigmoid(jnp.dot(xb, wg_ref[:, sl], preferred_element_type=F32) + bg_ref[:, sl])
        term = gate * jnp.dot(br, wb_ref[n], preferred_element_type=F32)
        merged = term if merged is None else merged + term
    y = DEEPNORM_ALPHA * x + jnp.dot(merged.astype(BF16), wo_ref[...], preferred_element_type=F32)
    o_ref[...] = _layer_norm(y, g1_ref[...], b1_ref[...])


def _merge_call(x, bra, brb, o2, cz, wg, bg, wb, wo, ng, g1, b1, tm=512):
    t = x.shape[0]
    row = lambda i: (i, 0)
    return pl.pallas_call(
        _merge_kernel,
        grid=(t // tm,),
        in_specs=[
            pl.BlockSpec((tm, D_MODEL), row),
            pl.BlockSpec((tm, 512), row),
            pl.BlockSpec((tm, 512), row),
            pl.BlockSpec((1, tm, 512), lambda i: (0, i, 0)),
            pl.BlockSpec((1, tm, 512), lambda i: (1, i, 0)),
            pl.BlockSpec((tm, 512), row),
            _const_spec(wg.shape), _const_spec(bg.shape), _const_spec(wb.shape), _const_spec(wo.shape),
            _const_spec(ng.shape), _const_spec(g1.shape), _const_spec(b1.shape),
        ],
        out_specs=pl.BlockSpec((tm, D_MODEL), row),
        out_shape=jax.ShapeDtypeStruct((t, D_MODEL), F32),
        compiler_params=_cparams(("parallel",)),
        name="merge",
    )(x, bra, brb, o2, o2, cz, wg, bg, wb, wo, ng, g1, b1)


def _ffn_kernel(x_ref, p_ref, wfg_ref, wfu_ref, wfo_ref, wpg_ref, bpg_ref, wpp_ref, g2_ref, b2_ref, o_ref):
    x = x_ref[...]
    xb = x.astype(BF16)
    gate = jnp.dot(xb, wfg_ref[...], preferred_element_type=F32)
    up = jnp.dot(xb, wfu_ref[...], preferred_element_type=F32)
    ffn = jnp.dot((_silu(gate) * up).astype(BF16), wfo_ref[...], preferred_element_type=F32)
    ple = (jax.nn.sigmoid(jnp.dot(xb, wpg_ref[...], preferred_element_type=F32) + bpg_ref[...])
           * jnp.dot(p_ref[...].astype(BF16), wpp_ref[...], preferred_element_type=F32))
    o_ref[...] = _layer_norm(DEEPNORM_ALPHA * x + ffn + ple, g2_ref[...], b2_ref[...])


def _ffn_call(x, p, wfg, wfu, wfo, wpg, bpg, wpp, g2, b2, tm=256):
    t = x.shape[0]
    row = lambda i: (i, 0)
    return pl.pallas_call(
        _ffn_kernel,
        grid=(t // tm,),
        in_specs=[
            pl.BlockSpec((tm, D_MODEL), row),
            pl.BlockSpec((tm, PLE_DIM), row),
            _const_spec(wfg.shape), _const_spec(wfu.shape), _const_spec(wfo.shape), _const_spec(wpg.shape),
            _const_spec(bpg.shape), _const_spec(wpp.shape), _const_spec(g2.shape), _const_spec(b2.shape),
        ],
        out_specs=pl.BlockSpec((tm, D_MODEL), row),
        out_shape=jax.ShapeDtypeStruct((t, D_MODEL), F32),
        compiler_params=_cparams(("parallel",)),
        name="ffn",
    )(x, p, wfg, wfu, wfo, wpg, bpg, wpp, g2, b2)


def _rope_tables(seq):
    t = jnp.arange(seq, dtype=jnp.int32)
    rowf = (t // GRID_W).astype(F32)
    colf = (t % GRID_W).astype(F32)
    inv_freq = ROPE_THETA ** (-jnp.arange(ROPE_PAIRS, dtype=F32) / ROPE_PAIRS)
    ang_r = rowf[:, None] * inv_freq[None, :]
    ang_c = colf[:, None] * inv_freq[None, :]
    cos = jnp.concatenate([jnp.cos(ang_r), jnp.cos(ang_r), jnp.cos(ang_c), jnp.cos(ang_c)], axis=1)
    sin = jnp.concatenate([-jnp.sin(ang_r), jnp.sin(ang_r), -jnp.sin(ang_c), jnp.sin(ang_c)], axis=1)
    return jnp.tile(cos, (1, 2)), jnp.tile(sin, (1, 2))


def _tile_lanes(v, reps):
    return jnp.tile(v.astype(F32), reps).reshape(1, -1)


def _layer(x, p_i, bsz, seq, tables, w):
    (w_in, qn_g, kn_g, rpb, conv_w, a_log, dt_bias, dn_g, w_gate, b_gate, w_branch, w_out, ln1_g, ln1_b,
     w_ffn_in, w_ffn_out, w_ple_gate, b_ple_gate, w_ple_proj, ln2_g, ln2_b) = w
    o = _OFF
    wk = [w_in[:, o[1] + HEAD_DIM * h:o[1] + HEAD_DIM * (h + 1)] for h in range(ATT_KV_HEADS)]
    wv = [w_in[:, o[2] + HEAD_DIM * h:o[2] + HEAD_DIM * (h + 1)] for h in range(ATT_KV_HEADS)]
    wa = jnp.concatenate([w_in[:, o[0]:o[1]], wk[0], wk[0], wk[1], wk[1], wv[0], wv[0], wv[1], wv[1]], axis=1).astype(BF16)
    wn = w_in[:, o[3]:o[6]].astype(BF16)
    wc = w_in[:, o[6]:o[9]].astype(BF16)
    wz = w_in[:, o[9]:o[10]].astype(BF16)
    wab = jnp.pad(w_in[:, o[10]:o[14]], ((0, 0), (0, LANES - 32))).astype(BF16)
    gp = jnp.zeros((8, LANES), F32)
    gp = gp.at[0, :16].set(a_log.reshape(16).astype(F32))
    gp = gp.at[1, :16].set(dt_bias.reshape(16).astype(F32))
    gp = gp.at[2, :16].set(1.0)
    cos_t, sin_t = tables
    (qa, kt, va, nq, nkt, nv, dq, dk, dv, cz, gn) = _proj_call(
        x, bsz, seq, cos_t, sin_t, wa, wn, wc, wz, wab,
        _tile_lanes(qn_g, 2), _tile_lanes(kn_g, 2), conv_w.astype(F32), gp)
    r3 = lambda a: a.reshape(bsz, seq, a.shape[-1])
    bra = _gqa_call(r3(qa), kt, r3(va))
    brb = _na_call(r3(nq), nkt, r3(nv), rpb)
    o2 = _dn_call(r3(dq), r3(dk), r3(dv), r3(gn), _dn_rows(gn, bsz, seq))
    t = bsz * seq
    x = _merge_call(x, bra.reshape(t, 512), brb.reshape(t, 512), o2.reshape(2, t, 512), cz,
                    w_gate.astype(BF16), b_gate.astype(F32).reshape(1, -1), w_branch.astype(BF16),
                    w_out.astype(BF16), _tile_lanes(dn_g, 2), ln1_g.astype(F32).reshape(1, -1),
                    ln1_b.astype(F32).reshape(1, -1))
    x = _ffn_call(x, p_i, w_ffn_in[:, :FFN_HIDDEN].astype(BF16), w_ffn_in[:, FFN_HIDDEN:].astype(BF16),
                  w_ffn_out.astype(BF16), w_ple_gate.astype(BF16), b_ple_gate.astype(F32).reshape(1, -1),
                  w_ple_proj.astype(BF16), ln2_g.astype(F32).reshape(1, -1), ln2_b.astype(F32).reshape(1, -1))
    return x


def _trunk(x, p, emb_ln_g, emb_ln_b, weights):
    bsz, seq, d = x.shape
    depth = p.shape[0]
    t = bsz * seq
    xf = _ln_call(x.reshape(t, d), emb_ln_g.astype(F32), emb_ln_b.astype(F32))
    tables = _rope_tables(seq)
    for i in range(depth):
        xf = _layer(xf, p[i].reshape(t, -1), bsz, seq, tables, tuple(a[i] for a in weights))
    return xf.reshape(bsz, seq, d)


def kernel(x_prompt, x_sample, p_prompt, p_sample, emb_ln_g, emb_ln_b, w_in, att_q_norm_g, att_k_norm_g, na_rpb, dn_conv_w, dn_a_log, dn_dt_bias, dn_norm_g, w_gate, b_gate, w_branch, w_out, ln1_g, ln1_b, w_ffn_in, w_ffn_out, w_ple_gate, b_ple_gate, w_ple_proj, ln2_g, ln2_b):
    weights = (w_in, att_q_norm_g, att_k_norm_g, na_rpb, dn_conv_w, dn_a_log, dn_dt_bias, dn_norm_g, w_gate,
               b_gate, w_branch, w_out, ln1_g, ln1_b, w_ffn_in, w_ffn_out, w_ple_gate, b_ple_gate, w_ple_proj,
               ln2_g, ln2_b)
    nb = x_prompt.shape[0]
    x = jnp.concatenate([x_prompt, x_sample], axis=0)
    p = jnp.concatenate([p_prompt, p_sample], axis=1)
    y = _trunk(x, p, emb_ln_g, emb_ln_b, weights)
    return (y[:nb], y[nb:])
```

```python
import functools

import numpy as np
import jax
import jax.numpy as jnp
from jax import lax
from jax.experimental import pallas as pl
from jax.experimental.pallas import tpu as pltpu

F32 = jnp.float32
BF16 = jnp.bfloat16

D_MODEL = 1024
DEPTH = 4
GRID_W = 64
HEAD_DIM = 64
ATT_Q_HEADS = 8
ATT_KV_HEADS = 2
ROPE_PAIRS = HEAD_DIM // 4
ROPE_THETA = 10000.0
NA_HEADS = 8
NA_WIN_ROWS = 8
NA_WIN_COLS = 16
DN_HEADS = 8
DN_CHUNK = 64
BRANCH_WIDTH = 512
FFN_HIDDEN = 2816
PLE_DIM = 256
LN_EPS = 1e-5
RMS_EPS = 1e-6
L2_EPS = 1e-6
DEEPNORM_ALPHA = (2 * DEPTH) ** 0.25

LANES = 128
HALO = 16
VMEM_LIMIT = 56 * 1024 * 1024
NEG = -1e30
LOG2E = 1.4426950408889634

_OFF = np.cumsum([0, 512, 128, 128, 512, 512, 512, 512, 512, 512, 512, 8, 8, 8, 8])


def _cparams(sem):
    return pltpu.CompilerParams(dimension_semantics=sem, vmem_limit_bytes=VMEM_LIMIT)


def _const_spec(shape):
    nd = len(shape)
    return pl.BlockSpec(shape, lambda *a: (0,) * nd, pipeline_mode=pl.Buffered(1))


def _bdot(a, b):
    return jnp.dot(a.astype(BF16), b.astype(BF16), preferred_element_type=F32)


def _split(x, n):
    out = []
    r = x
    for _ in range(n):
        p = r.astype(BF16)
        out.append(p)
        r = r - p.astype(F32)
    return out


def _dot_exact_rhs(a, b01, n=3):
    acc = None
    for p in _split(a, n):
        t = jnp.dot(p, b01, preferred_element_type=F32)
        acc = t if acc is None else acc + t
    return acc


def _dot_exact_lhs(a01, b, n=3):
    acc = None
    for p in _split(b, n):
        t = jnp.dot(a01, p, preferred_element_type=F32)
        acc = t if acc is None else acc + t
    return acc


def _seg_ones(n, seg):
    r = lax.broadcasted_iota(jnp.int32, (n, n), 0) // seg
    c = lax.broadcasted_iota(jnp.int32, (n, n), 1) // seg
    return jnp.where(r == c, 1.0, 0.0).astype(BF16)


def _seg_sum(x, ones):
    return _dot_exact_rhs(x, ones, 2)


def _layer_norm(y, g, b):
    mu = jnp.mean(y, axis=-1, keepdims=True)
    yc = y - mu
    var = jnp.mean(yc * yc, axis=-1, keepdims=True)
    return yc * lax.rsqrt(var + LN_EPS) * g + b


def _silu(x):
    return x * jax.nn.sigmoid(x)


def _ln_kernel(x_ref, g_ref, b_ref, o_ref):
    o_ref[...] = _layer_norm(x_ref[...], g_ref[...], b_ref[...])


def _ln_call(x, g, b, tm=1024):
    t, d = x.shape
    return pl.pallas_call(
        _ln_kernel,
        grid=(t // tm,),
        in_specs=[pl.BlockSpec((tm, d), lambda i: (i, 0)), _const_spec((1, d)), _const_spec((1, d))],
        out_specs=pl.BlockSpec((tm, d), lambda i: (i, 0)),
        out_shape=jax.ShapeDtypeStruct((t, d), F32),
        compiler_params=_cparams(("parallel",)),
        name="emb_ln",
    )(x, g.reshape(1, d), b.reshape(1, d))


def _proj_kernel(xp_ref, x_ref, xn_ref, cos_ref, sin_ref, wa_ref, wn_ref, wc_ref, wz_ref, wab_ref,
                 qg_ref, kg_ref, cw_ref, gp_ref,
                 qa_ref, kt_ref, va_ref, nq_ref, nkt_ref, nv_ref, dq_ref, dk_ref, dv_ref, cz_ref, gn_ref,
                 xs_ref, c_ref, *, tm, nts):
    i = pl.program_id(0)
    first = (i % nts) == 0
    last = (i % nts) == nts - 1
    xs_ref[0:HALO, :] = jnp.where(first, 0.0, xp_ref[...]).astype(BF16)
    xs_ref[HALO:HALO + tm, :] = x_ref[...].astype(BF16)
    xs_ref[HALO + tm:, :] = jnp.where(last, 0.0, xn_ref[...]).astype(BF16)
    xm = xs_ref[HALO:HALO + tm, :]

    ones = _seg_ones(LANES, HEAD_DIM)
    cos = cos_ref[...]
    sin = sin_ref[...]
    lane = lax.broadcasted_iota(jnp.int32, (tm, LANES), 1)
    first_half = (lane % (2 * ROPE_PAIRS)) < ROPE_PAIRS

    def norm_rope(xq, g):
        ss = _seg_sum(xq * xq, ones)
        xn = xq * lax.rsqrt(ss * (1.0 / HEAD_DIM) + RMS_EPS) * g
        partner = jnp.where(first_half, pltpu.roll(xn, LANES - ROPE_PAIRS, 1), pltpu.roll(xn, ROPE_PAIRS, 1))
        return xn * cos + partner * sin

    ya = jnp.dot(xm, wa_ref[...], preferred_element_type=F32)
    scale = HEAD_DIM ** -0.5
    scale2 = scale * LOG2E
    for j in range(4):
        sl = slice(LANES * j, LANES * (j + 1))
        qa_ref[:, sl] = (norm_rope(ya[:, sl], qg_ref[...]) * scale2).astype(BF16)
    for j in range(2):
        sl = slice(512 + LANES * j, 512 + LANES * (j + 1))
        kt_ref[0, LANES * j:LANES * (j + 1), :] = norm_rope(ya[:, sl], kg_ref[...]).T.astype(BF16)
    lane_v = lax.broadcasted_iota(jnp.int32, (tm, 2 * LANES), 1)
    va_ref[...] = jnp.where(lane_v % LANES < HEAD_DIM, ya[:, 768:1024], 1.0).astype(BF16)

    yn = jnp.dot(xm, wn_ref[...], preferred_element_type=F32)
    nq_ref[...] = (yn[:, 0:512] * scale2).astype(BF16)
    nkt_ref[0] = yn[:, 512:1024].T.astype(BF16)
    nv_ref[...] = yn[:, 1024:1536].astype(BF16)

    c_ref[...] = jnp.dot(xs_ref[...], wc_ref[...], preferred_element_type=F32)
    for j in range(12):
        sl = slice(LANES * j, LANES * (j + 1))
        y = (c_ref[HALO - 1:HALO - 1 + tm, sl] * cw_ref[0:1, sl]
             + c_ref[HALO:HALO + tm, sl] * cw_ref[1:2, sl]
             + c_ref[HALO + 1:HALO + 1 + tm, sl] * cw_ref[2:3, sl])
        y = _silu(y)
        if j < 8:
            y = y * lax.rsqrt(_seg_sum(y * y, ones) + L2_EPS)
        if j < 4:
            y = y * scale
        osl = slice(LANES * (j % 4), LANES * (j % 4 + 1))
        (dq_ref, dk_ref, dv_ref)[j // 4][:, osl] = y
    cz_ref[...] = jnp.dot(xm, wz_ref[...], preferred_element_type=F32)

    raw = jnp.dot(xm, wab_ref[...], preferred_element_type=F32)
    a_log = gp_ref[0:1, :]
    dt_b = gp_ref[1:2, :]
    is_g = gp_ref[2:3, :] > 0.5
    z = raw + dt_b
    softplus = jnp.maximum(z, 0.0) + jnp.log(1.0 + jnp.exp(-jnp.abs(z)))
    gn_ref[...] = jnp.where(is_g, -jnp.exp(a_log) * softplus, jax.nn.sigmoid(raw))


def _proj_call(x, bsz, seq, cos_t, sin_t, wa, wn, wc, wz, wab, qg, kg, cw, gp, tm=512):
    t = x.shape[0]
    nts = seq // tm
    nh = t // HALO
    row = lambda i: (i, 0)
    seqrow = lambda i: (i % nts, 0)
    in_specs = [
        pl.BlockSpec((HALO, D_MODEL), lambda i: (jnp.maximum(i * (tm // HALO) - 1, 0), 0)),
        pl.BlockSpec((tm, D_MODEL), row),
        pl.BlockSpec((HALO, D_MODEL), lambda i: (jnp.minimum((i + 1) * (tm // HALO), nh - 1), 0)),
        pl.BlockSpec((tm, LANES), seqrow),
        pl.BlockSpec((tm, LANES), seqrow),
        _const_spec(wa.shape), _const_spec(wn.shape), _const_spec(wc.shape), _const_spec(wz.shape),
        _const_spec(wab.shape), _const_spec(qg.shape), _const_spec(kg.shape), _const_spec(cw.shape),
        _const_spec(gp.shape),
    ]
    tposed = lambda i: (i // nts, 0, i % nts)
    out_specs = [
        pl.BlockSpec((tm, 512), row),
        pl.BlockSpec((1, 256, tm), tposed),
        pl.BlockSpec((tm, 256), row),
        pl.BlockSpec((tm, 512), row),
        pl.BlockSpec((1, 512, tm), tposed),
        pl.BlockSpec((tm, 512), row),
        pl.BlockSpec((tm, 512), row),
        pl.BlockSpec((tm, 512), row),
        pl.BlockSpec((tm, 512), row),
        pl.BlockSpec((tm, 512), row),
        pl.BlockSpec((tm, LANES), row),
    ]
    out_shape = [
        jax.ShapeDtypeStruct((t, 512), BF16),
        jax.ShapeDtypeStruct((bsz, 256, seq), BF16),
        jax.ShapeDtypeStruct((t, 256), BF16),
        jax.ShapeDtypeStruct((t, 512), BF16),
        jax.ShapeDtypeStruct((bsz, 512, seq), BF16),
        jax.ShapeDtypeStruct((t, 512), BF16),
        jax.ShapeDtypeStruct((t, 512), F32),
        jax.ShapeDtypeStruct((t, 512), F32),
        jax.ShapeDtypeStruct((t, 512), F32),
        jax.ShapeDtypeStruct((t, 512), F32),
        jax.ShapeDtypeStruct((t, LANES), F32),
    ]
    return pl.pallas_call(
        functools.partial(_proj_kernel, tm=tm, nts=nts),
        grid=(t // tm,),
        in_specs=in_specs,
        out_specs=out_specs,
        out_shape=out_shape,
        scratch_shapes=[pltpu.VMEM((tm + 2 * HALO, D_MODEL), BF16), pltpu.VMEM((tm + 2 * HALO, 1536), F32)],
        compiler_params=_cparams(("parallel",)),
        name="proj",
    )(x, x, x, cos_t, sin_t, wa, wn, wc, wz, wab, qg, kg, cw, gp)


def _gqa_kernel(q_ref, kt_ref, v_ref, o_ref, *, tq):
    lane = lax.broadcasted_iota(jnp.int32, (tq, LANES), 1)
    lo = lane < HEAD_DIM
    lo4 = lax.broadcasted_iota(jnp.int32, (4 * tq, LANES), 1) < HEAD_DIM
    for h in range(ATT_KV_HEADS):
        parts = []
        for pp in range(2):
            qp = q_ref[0, :, 256 * h + LANES * pp:256 * h + LANES * (pp + 1)]
            zero = jnp.zeros_like(qp)
            parts.append(jnp.where(lo, qp, zero))
            parts.append(jnp.where(lo, zero, qp))
        q4 = jnp.concatenate(parts, axis=0)
        s = jnp.dot(q4, kt_ref[0, LANES * h:LANES * (h + 1), :], preferred_element_type=F32)
        m = jnp.max(s, axis=-1, keepdims=True)
        p = jnp.exp2(s - m).astype(BF16)
        pv = jnp.dot(p, v_ref[0, :, LANES * h:LANES * (h + 1)], preferred_element_type=F32)
        swapped = pltpu.roll(pv, HEAD_DIM, 1)
        inv_l = 1.0 / jnp.where(lo4, swapped, pv)
        on = pv * inv_l
        on_swapped = swapped * inv_l
        for pp in range(2):
            rows_even = slice((2 * pp) * tq, (2 * pp + 1) * tq)
            rows_odd = slice((2 * pp + 1) * tq, (2 * pp + 2) * tq)
            o_ref[0, :, 256 * h + LANES * pp:256 * h + LANES * (pp + 1)] = jnp.where(
                lo, on[rows_even], on_swapped[rows_odd]).astype(BF16)


def _gqa_call(qa, kt, va, tq=64):
    bsz, seq, _ = qa.shape
    return pl.pallas_call(
        functools.partial(_gqa_kernel, tq=tq),
        grid=(bsz, seq // tq),
        in_specs=[
            pl.BlockSpec((1, tq, 512), lambda b, i: (b, i, 0)),
            pl.BlockSpec((1, 256, seq), lambda b, i: (b, 0, 0)),
            pl.BlockSpec((1, seq, 256), lambda b, i: (b, 0, 0)),
        ],
        out_specs=pl.BlockSpec((1, tq, 512), lambda b, i: (b, i, 0)),
        out_shape=jax.ShapeDtypeStruct((bsz, seq, 512), BF16),
        compiler_params=_cparams(("parallel", "arbitrary")),
        name="gqa",
    )(qa, kt, va)


NA_QBLK = 128


def _na_geometry(seq):
    rows = seq // GRID_W
    wr = min(NA_WIN_ROWS, rows)
    wc = NA_WIN_COLS
    qrows = NA_QBLK // GRID_W
    nblk = seq // NA_QBLK
    nkb = min(5, nblk)
    krows = nkb * qrows
    starts = np.clip(np.arange(nblk) - 2, 0, nblk - nkb)
    variants, var_of = [], []
    for i in range(nblk):
        r = i * qrows + np.arange(qrows)
        r0 = np.clip(r - wr // 2, 0, rows - wr)
        kr = starts[i] * qrows + np.arange(krows)
        valid = (kr[None] >= r0[:, None]) & (kr[None] < r0[:, None] + wr)
        assert (valid.sum(1) == wr).all()
        dr = kr[None] - r[:, None] + (NA_WIN_ROWS - 1)
        sel = (valid[..., None] & (dr[..., None] == np.arange(2 * NA_WIN_ROWS - 1))).astype(np.float32)
        key = sel.tobytes()
        if key not in variants:
            variants.append(key)
        var_of.append(variants.index(key))
    row_sel = np.stack([np.frombuffer(k, np.float32).reshape(qrows, krows, 2 * NA_WIN_ROWS - 1) for k in variants])
    c = np.arange(GRID_W)
    c0 = np.clip(c - wc // 2, 0, GRID_W - wc)
    validc = (c[None] >= c0[:, None]) & (c[None] < c0[:, None] + wc)
    dc = c[None] - c[:, None] + (NA_WIN_COLS - 1)
    col_sel = (validc[..., None] & (dc[..., None] == np.arange(2 * NA_WIN_COLS - 1))).astype(np.float32)
    return nkb, starts.astype(np.int32), np.asarray(var_of, np.int32), row_sel, col_sel


def _na_bias_table(rpb, row_sel, col_sel):
    hi = lax.Precision.HIGHEST
    cols = jnp.einsum('hrd,cnd->hrcn', rpb, col_sel, precision=hi)
    tab = jnp.einsum('vqkr,hrcn->vhqckn', row_sel, cols, precision=hi)
    valid = (row_sel.sum(-1) > 0)[:, None, :, None, :, None] & (col_sel.sum(-1) > 0)[None, None, None, :, None, :]
    tab = jnp.where(valid, tab * LOG2E, NEG)
    nv, h = tab.shape[:2]
    return tab.reshape(nv, h // 2, 2 * NA_QBLK, -1)


def _na_kernel(var_ref, kb_ref, q_ref, *refs, nkb):
    kt_refs = refs[:nkb]
    v_refs = refs[nkb:2 * nkb]
    bias_ref = refs[2 * nkb]
    o_ref = refs[2 * nkb + 1]
    lane = lax.broadcasted_iota(jnp.int32, (NA_QBLK, LANES), 1)
    lo = lane < HEAD_DIM
    for pp in range(NA_HEADS // 2):
        sl = slice(LANES * pp, LANES * (pp + 1))
        qp = q_ref[0, :, sl]
        zero = jnp.zeros_like(qp)
        q2 = jnp.concatenate([jnp.where(lo, qp, zero), jnp.where(lo, zero, qp)], axis=0)
        ktp = jnp.concatenate([r[0, sl, :] for r in kt_refs], axis=1)
        vp = jnp.concatenate([r[0, :, sl] for r in v_refs], axis=0)
        s = jnp.dot(q2, ktp, preferred_element_type=F32) + bias_ref[0, pp]
        m = jnp.max(s, axis=-1, keepdims=True)
        p = jnp.exp2(s - m)
        l = jnp.sum(p, axis=-1, keepdims=True)
        pv = jnp.dot(p.astype(BF16), vp, preferred_element_type=F32) / l
        o_ref[0, :, sl] = jnp.where(lo, pv[:NA_QBLK], pv[NA_QBLK:]).astype(BF16)


def _na_call(nq, nkt, nv, rpb):
    bsz, seq, _ = nq.shape
    nkb, starts, var_of, row_sel, col_sel = _na_geometry(seq)
    bias = _na_bias_table(rpb.astype(F32), row_sel, col_sel)
    nblk = seq // NA_QBLK
    in_specs = [pl.BlockSpec((1, NA_QBLK, 512), lambda b, i, var, kb: (b, i, 0))]
    for j in range(nkb):
        in_specs.append(pl.BlockSpec((1, 512, NA_QBLK), lambda b, i, var, kb, j=j: (b, 0, kb[i] + j)))
    for j in range(nkb):
        in_specs.append(pl.BlockSpec((1, NA_QBLK, 512), lambda b, i, var, kb, j=j: (b, kb[i] + j, 0)))
    in_specs.append(pl.BlockSpec((1,) + bias.shape[1:], lambda b, i, var, kb: (var[i], 0, 0, 0)))
    grid_spec = pltpu.PrefetchScalarGridSpec(
        num_scalar_prefetch=2,
        grid=(bsz, nblk),
        in_specs=in_specs,
        out_specs=pl.BlockSpec((1, NA_QBLK, 512), lambda b, i, var, kb: (b, i, 0)),
    )
    return pl.pallas_call(
        functools.partial(_na_kernel, nkb=nkb),
        grid_spec=grid_spec,
        out_shape=jax.ShapeDtypeStruct((bsz, seq, 512), BF16),
        compiler_params=_cparams(("parallel", "arbitrary")),
        name="na",
    )(jnp.asarray(var_of), jnp.asarray(starts), nq, *([nkt] * nkb), *([nv] * nkb), bias)


DN_TC = 512
DN_NHP = 2


def _bd(x, lo):
    xb = x.astype(BF16)
    zero = jnp.zeros_like(xb)
    return jnp.concatenate([jnp.where(lo, xb, zero), jnp.where(lo, zero, xb)], axis=-2)


def _bmm(a, b16):
    return jnp.einsum('bik,bkj->bij', a.astype(BF16), b16, preferred_element_type=F32)


def _dn_prepare(d, hp, q, k, v, gn, gr):
    C = DN_CHUNK
    nch = DN_TC // C
    sgn = 1 - 2 * d

    col = lax.broadcasted_iota(jnp.int32, (LANES, 2 * LANES), 0)
    ln2 = lax.broadcasted_iota(jnp.int32, (LANES, 2 * LANES), 1)
    want = (ln2 // LANES) * 16 + d * 8 + 2 * hp + (ln2 % LANES) // HEAD_DIM
    onehot = jnp.where(col == want, 1.0, 0.0).astype(BF16)
    gb = _dot_exact_rhs(gn, onehot)
    g_col = gb[:, :LANES]
    beta = gb[:, LANES:]
    r = lax.broadcasted_iota(jnp.int32, (DN_TC, DN_TC), 0)
    cc = lax.broadcasted_iota(jnp.int32, (DN_TC, DN_TC), 1)
    same_chunk = r // C == cc // C
    tri = jnp.where(same_chunk & ((r - cc) * sgn >= 0), 1.0, 0.0).astype(BF16)
    blk = jnp.where(same_chunk, 1.0, 0.0).astype(BF16)
    g_pieces = _split(g_col, 3)
    gc = sum(jnp.dot(tri, p, preferred_element_type=F32) for p in g_pieces)
    tot = sum(jnp.dot(blk, p, preferred_element_type=F32) for p in g_pieces)
    r2 = lax.broadcasted_iota(jnp.int32, (LANES, LANES), 0)
    c2 = lax.broadcasted_iota(jnp.int32, (LANES, LANES), 1)
    tri2 = jnp.where((r2 // C == c2 // C) & ((c2 - r2) * sgn >= 0), 1.0, 0.0).astype(BF16)
    gcr = _dot_exact_rhs(gr, tri2)

    egc = jnp.exp(gc)
    kbeta = k * beta
    vbeta = v * beta
    kbd = kbeta * egc
    qd = q * egc
    kd = k * jnp.exp(tot - gc)
    etot = jnp.exp(tot)

    row = lax.broadcasted_iota(jnp.int32, (C, LANES), 0)
    lane = lax.broadcasted_iota(jnp.int32, (C, LANES), 1)
    lo = lane < HEAD_DIM
    jj = lane % HEAD_DIM
    earlier_eq = (row - jj) * sgn >= 0
    earlier = (row - jj) * sgn > 0
    eye2 = jnp.where(row == jj, 1.0, 0.0)

    def chunks(a):
        return a.reshape(nch, C, a.shape[-1])

    q3, k3 = chunks(q), chunks(k)
    gc3 = chunks(gc)
    diff = jnp.stack([gc3[j] - gcr[j:j + 1, :] for j in range(nch)])
    decay = jnp.where(earlier_eq, jnp.exp(jnp.minimum(diff, 0.0)), 0.0)
    ai = jnp.einsum('bik,bjk->bij', jnp.concatenate([chunks(kbeta), q3], axis=1).astype(BF16), _bd(k3, lo),
                    preferred_element_type=F32)
    intra = ai[:, C:] * decay
    x = jnp.where(earlier, -ai[:, :C] * decay, 0.0)
    t = eye2 + x
    p = _bmm(x, _bd(x, lo))
    for lvl in range(5):
        pbd = _bd(p, lo)
        if lvl < 4:
            rr = _bmm(jnp.concatenate([p, t], axis=1), pbd)
            p = rr[:, :C]
            t = t + rr[:, C:]
        else:
            t = t + _bmm(t, pbd)
    uw = _bmm(t, jnp.concatenate([_bd(chunks(vbeta), lo), _bd(chunks(kbd), lo)], axis=2))
    return uw, intra, chunks(qd), chunks(kd), chunks(etot)


def _dn_kernel(qf_ref, kf_ref, vf_ref, gnf_ref, grf_ref, qb_ref, kb_ref, vb_ref, gnb_ref, grb_ref,
               of_ref, ob_ref, s_ref):
    C = DN_CHUNK
    nch = DN_TC // C

    @pl.when(pl.program_id(2) == 0)
    def _():
        s_ref[...] = jnp.zeros_like(s_ref)

    in_refs = ((qf_ref, kf_ref, vf_ref, gnf_ref, grf_ref), (qb_ref, kb_ref, vb_ref, gnb_ref, grb_ref))
    out_refs = (of_ref, ob_ref)
    lo = lax.broadcasted_iota(jnp.int32, (C, LANES), 1) < HEAD_DIM
    r3 = lax.broadcasted_iota(jnp.int32, (LANES, LANES), 0) // HEAD_DIM
    c3 = lax.broadcasted_iota(jnp.int32, (LANES, LANES), 1) // HEAD_DIM
    same_head = r3 == c3

    chains = []
    for d in range(2):
        q_ref, k_ref, v_ref, gn_ref, gr_ref = in_refs[d]
        for i in range(DN_NHP):
            sl = slice(LANES * i, LANES * (i + 1))
            hp = pl.program_id(1) * DN_NHP + i
            prep = _dn_prepare(d, hp, q_ref[0, :, sl], k_ref[0, :, sl], v_ref[0, :, sl], gn_ref[0], gr_ref[0, i, 0])
            chains.append((d, i, sl, prep))
    states = [s_ref[d, i] for d, i, _, _ in chains]
    for step in range(nch):
        for n, (d, i, sl, (uw, intra, qd, kd, etot)) in enumerate(chains):
            j = step if d == 0 else nch - 1 - step
            wq = jnp.dot(jnp.concatenate([uw[j, :, LANES:], qd[j]], axis=0).astype(BF16), states[n].astype(BF16),
                         preferred_element_type=F32)
            v_new = uw[j, :, :LANES] - wq[:C]
            out_refs[d][0, C * j:C * (j + 1), sl] = wq[C:] + jnp.dot(intra[j].astype(BF16), _bd(v_new, lo),
                                                                      preferred_element_type=F32)
            upd = jnp.dot(kd[j].T.astype(BF16), v_new.astype(BF16), preferred_element_type=F32)
            states[n] = states[n] * etot[j, 0:1] + jnp.where(same_head, upd, 0.0)
    for n, (d, i, _, _) in enumerate(chains):
        s_ref[d, i] = states[n]


def _dn_call(dq, dk, dv, gn, gr):
    bsz, seq, _ = dq.shape
    nsc = seq // DN_TC
    nch = DN_TC // DN_CHUNK
    width = LANES * DN_NHP
    in_specs = []
    for d in range(2):
        tmap = (lambda c: c) if d == 0 else (lambda c: nsc - 1 - c)
        qkv_spec = pl.BlockSpec((1, DN_TC, width), lambda b, hp, c, tmap=tmap: (b, tmap(c), hp))
        in_specs += [
            qkv_spec, qkv_spec, qkv_spec,
            pl.BlockSpec((1, DN_TC, LANES), lambda b, hp, c, tmap=tmap: (b, tmap(c), 0)),
            pl.BlockSpec((1, DN_NHP, 1, nch, LANES), lambda b, hp, c, tmap=tmap, d=d: (d, hp, b, tmap(c), 0)),
        ]
    out_specs = [
        pl.BlockSpec((1, DN_TC, width), lambda b, hp, c: (b, c, hp)),
        pl.BlockSpec((1, DN_TC, width), lambda b, hp, c: (b, nsc - 1 - c, hp)),
    ]
    return pl.pallas_call(
        _dn_kernel,
        grid=(bsz, DN_HEADS // 2 // DN_NHP, nsc),
        in_specs=in_specs,
        out_specs=out_specs,
        out_shape=[jax.ShapeDtypeStruct((bsz, seq, 512), F32)] * 2,
        scratch_shapes=[pltpu.VMEM((2, DN_NHP, LANES, LANES), F32)],
        compiler_params=_cparams(("parallel", "parallel", "arbitrary")),
        name="deltanet",
    )(dq, dk, dv, gn, gr, dq, dk, dv, gn, gr)


def _dn_rows(gn, bsz, seq):
    g = gn[:, :32].reshape(bsz, seq // DN_CHUNK, DN_CHUNK, 2, 2, DN_HEADS // 2, 2)
    g = g[:, :, :, 0]
    g = jnp.transpose(g, (3, 4, 0, 1, 5, 2))
    return g.reshape(2, DN_HEADS // 2, bsz, seq // DN_CHUNK, LANES)


def _merge_kernel(x_ref, bra_ref, brb_ref, of_ref, ob_ref, cz_ref, wg_ref, bg_ref, wb_ref, wo_ref,
                  ng_ref, g1_ref, b1_ref, o_ref):
    x = x_ref[...]
    xb = x.astype(BF16)
    ones = _seg_ones(LANES, HEAD_DIM)
    parts = []
    for j in range(4):
        sl = slice(LANES * j, LANES * (j + 1))
        o = of_ref[:, sl] + ob_ref[:, sl]
        ss = _seg_sum(o * o, ones)
        parts.append((o * lax.rsqrt(ss * (1.0 / HEAD_DIM) + RMS_EPS) * ng_ref[...] * _silu(cz_ref[:, sl])).astype(BF16))
    brc = jnp.concatenate(parts, axis=1)
    merged = None
    for n, br in enumerate((bra_ref[...], brb_ref[...], brc)):
        sl = slice(D_MODEL * n, D_MODEL * (n + 1))
        gate = jax.nn.sigmoid(jnp.dot(xb, wg_ref[:, sl], preferred_element_type=F32) + bg_ref[:, sl])
        term = gate * jnp.dot(br, wb_ref[n], preferred_element_type=F32)
        merged = term if merged is None else merged + term
    y = DEEPNORM_ALPHA * x + jnp.dot(merged.astype(BF16), wo_ref[...], preferred_element_type=F32)
    o_ref[...] = _layer_norm(y, g1_ref[...], b1_ref[...])


def _merge_call(x, bra, brb, o_f, o_b, cz, wg, bg, wb, wo, ng, g1, b1, tm=512):
    t = x.shape[0]
    row = lambda i: (i, 0)
    return pl.pallas_call(
        _merge_kernel,
        grid=(t // tm,),
        in_specs=[
            pl.BlockSpec((tm, D_MODEL), row),
            pl.BlockSpec((tm, 512), row),
            pl.BlockSpec((tm, 512), row),
            pl.BlockSpec((tm, 512), row),
            pl.BlockSpec((tm, 512), row),
            pl.BlockSpec((tm, 512), row),
            _const_spec(wg.shape), _const_spec(bg.shape), _const_spec(wb.shape), _const_spec(wo.shape),
            _const_spec(ng.shape), _const_spec(g1.shape), _const_spec(b1.shape),
        ],
        out_specs=pl.BlockSpec((tm, D_MODEL), row),
        out_shape=jax.ShapeDtypeStruct((t, D_MODEL), F32),
        compiler_params=_cparams(("parallel",)),
        name="merge",
    )(x, bra, brb, o_f, o_b, cz, wg, bg, wb, wo, ng, g1, b1)


def _ffn_kernel(x_ref, p_ref, wfg_ref, wfu_ref, wfo_ref, wpg_ref, bpg_ref, wpp_ref, g2_ref, b2_ref, o_ref):
    x = x_ref[...]
    xb = x.astype(BF16)
    gate = jnp.dot(xb, wfg_ref[...], preferred_element_type=F32)
    up = jnp.dot(xb, wfu_ref[...], preferred_element_type=F32)
    ffn = jnp.dot((_silu(gate) * up).astype(BF16), wfo_ref[...], preferred_element_type=F32)
    ple = (jax.nn.sigmoid(jnp.dot(xb, wpg_ref[...], preferred_element_type=F32) + bpg_ref[...])
           * jnp.dot(p_ref[...].astype(BF16), wpp_ref[...], preferred_element_type=F32))
    o_ref[...] = _layer_norm(DEEPNORM_ALPHA * x + ffn + ple, g2_ref[...], b2_ref[...])


def _ffn_call(x, p, wfg, wfu, wfo, wpg, bpg, wpp, g2, b2, tm=256):
    t = x.shape[0]
    row = lambda i: (i, 0)
    return pl.pallas_call(
        _ffn_kernel,
        grid=(t // tm,),
        in_specs=[
            pl.BlockSpec((tm, D_MODEL), row),
            pl.BlockSpec((tm, PLE_DIM), row),
            _const_spec(wfg.shape), _const_spec(wfu.shape), _const_spec(wfo.shape), _const_spec(wpg.shape),
            _const_spec(bpg.shape), _const_spec(wpp.shape), _const_spec(g2.shape), _const_spec(b2.shape),
        ],
        out_specs=pl.BlockSpec((tm, D_MODEL), row),
        out_shape=jax.ShapeDtypeStruct((t, D_MODEL), F32),
        compiler_params=_cparams(("parallel",)),
        name="ffn",
    )(x, p, wfg, wfu, wfo, wpg, bpg, wpp, g2, b2)


def _rope_tables(seq):
    t = jnp.arange(seq, dtype=jnp.int32)
    rowf = (t // GRID_W).astype(F32)
    colf = (t % GRID_W).astype(F32)
    inv_freq = ROPE_THETA ** (-jnp.arange(ROPE_PAIRS, dtype=F32) / ROPE_PAIRS)
    ang_r = rowf[:, None] * inv_freq[None, :]
    ang_c = colf[:, None] * inv_freq[None, :]
    cos = jnp.concatenate([jnp.cos(ang_r), jnp.cos(ang_r), jnp.cos(ang_c), jnp.cos(ang_c)], axis=1)
    sin = jnp.concatenate([-jnp.sin(ang_r), jnp.sin(ang_r), -jnp.sin(ang_c), jnp.sin(ang_c)], axis=1)
    return jnp.tile(cos, (1, 2)), jnp.tile(sin, (1, 2))


def _tile_lanes(v, reps):
    return jnp.tile(v.astype(F32), reps).reshape(1, -1)


def _layer(x, p_i, bsz, seq, tables, w):
    (w_in, qn_g, kn_g, rpb, conv_w, a_log, dt_bias, dn_g, w_gate, b_gate, w_branch, w_out, ln1_g, ln1_b,
     w_ffn_in, w_ffn_out, w_ple_gate, b_ple_gate, w_ple_proj, ln2_g, ln2_b) = w
    o = _OFF
    wk = [w_in[:, o[1] + HEAD_DIM * h:o[1] + HEAD_DIM * (h + 1)] for h in range(ATT_KV_HEADS)]
    wv = [w_in[:, o[2] + HEAD_DIM * h:o[2] + HEAD_DIM * (h + 1)] for h in range(ATT_KV_HEADS)]
    wa = jnp.concatenate([w_in[:, o[0]:o[1]], wk[0], wk[0], wk[1], wk[1], wv[0], wv[0], wv[1], wv[1]], axis=1).astype(BF16)
    wn = w_in[:, o[3]:o[6]].astype(BF16)
    wc = w_in[:, o[6]:o[9]].astype(BF16)
    wz = w_in[:, o[9]:o[10]].astype(BF16)
    wab = jnp.pad(w_in[:, o[10]:o[14]], ((0, 0), (0, LANES - 32))).astype(BF16)
    gp = jnp.zeros((8, LANES), F32)
    gp = gp.at[0, :16].set(a_log.reshape(16).astype(F32))
    gp = gp.at[1, :16].set(dt_bias.reshape(16).astype(F32))
    gp = gp.at[2, :16].set(1.0)
    cos_t, sin_t = tables
    (qa, kt, va, nq, nkt, nv, dq, dk, dv, cz, gn) = _proj_call(
        x, bsz, seq, cos_t, sin_t, wa, wn, wc, wz, wab,
        _tile_lanes(qn_g, 2), _tile_lanes(kn_g, 2), conv_w.astype(F32), gp)
    r3 = lambda a: a.reshape(bsz, seq, a.shape[-1])
    bra = _gqa_call(r3(qa), kt, r3(va))
    brb = _na_call(r3(nq), nkt, r3(nv), rpb)
    o_f, o_b = _dn_call(r3(dq), r3(dk), r3(dv), r3(gn), _dn_rows(gn, bsz, seq))
    t = bsz * seq
    x = _merge_call(x, bra.reshape(t, 512), brb.reshape(t, 512), o_f.reshape(t, 512), o_b.reshape(t, 512), cz,
                    w_gate.astype(BF16), b_gate.astype(F32).reshape(1, -1), w_branch.astype(BF16),
                    w_out.astype(BF16), _tile_lanes(dn_g, 2), ln1_g.astype(F32).reshape(1, -1),
                    ln1_b.astype(F32).reshape(1, -1))
    x = _ffn_call(x, p_i, w_ffn_in[:, :FFN_HIDDEN].astype(BF16), w_ffn_in[:, FFN_HIDDEN:].astype(BF16),
                  w_ffn_out.astype(BF16), w_ple_gate.astype(BF16), b_ple_gate.astype(F32).reshape(1, -1),
                  w_ple_proj.astype(BF16), ln2_g.astype(F32).reshape(1, -1), ln2_b.astype(F32).reshape(1, -1))
    return x


def _trunk(x, p, emb_ln_g, emb_ln_b, weights):
    bsz, seq, d = x.shape
    depth = p.shape[0]
    t = bsz * seq
    xf = _ln_call(x.reshape(t, d), emb_ln_g.astype(F32), emb_ln_b.astype(F32))
    tables = _rope_tables(seq)
    for i in range(depth):
        xf = _layer(xf, p[i].reshape(t, -1), bsz, seq, tables, tuple(a[i] for a in weights))
    return xf.reshape(bsz, seq, d)


def kernel(x_prompt, x_sample, p_prompt, p_sample, emb_ln_g, emb_ln_b, w_in, att_q_norm_g, att_k_norm_g, na_rpb, dn_conv_w, dn_a_log, dn_dt_bias, dn_norm_g, w_gate, b_gate, w_branch, w_out, ln1_g, ln1_b, w_ffn_in, w_ffn_out, w_ple_gate, b_ple_gate, w_ple_proj, ln2_g, ln2_b):
    weights = (w_in, att_q_norm_g, att_k_norm_g, na_rpb, dn_conv_w, dn_a_log, dn_dt_bias, dn_norm_g, w_gate,
               b_gate, w_branch, w_out, ln1_g, ln1_b, w_ffn_in, w_ffn_out, w_ple_gate, b_ple_gate, w_ple_proj,
               ln2_g, ln2_b)
    nb = x_prompt.shape[0]
    x = jnp.concatenate([x_prompt, x_sample], axis=0)
    p = jnp.concatenate([p_prompt, p_sample], axis=1)
    y = _trunk(x, p, emb_ln_g, emb_ln_b, weights)
    return (y[:nb], y[nb:])
```

```python
import functools

import numpy as np
import jax
import jax.numpy as jnp
from jax import lax
from jax.experimental import pallas as pl
from jax.experimental.pallas import tpu as pltpu

F32 = jnp.float32
BF16 = jnp.bfloat16

D_MODEL = 1024
DEPTH = 4
GRID_W = 64
HEAD_DIM = 64
ATT_Q_HEADS = 8
ATT_KV_HEADS = 2
ROPE_PAIRS = HEAD_DIM // 4
ROPE_THETA = 10000.0
NA_HEADS = 8
NA_WIN_ROWS = 8
NA_WIN_COLS = 16
DN_HEADS = 8
DN_CHUNK = 64
BRANCH_WIDTH = 512
FFN_HIDDEN = 2816
PLE_DIM = 256
LN_EPS = 1e-5
RMS_EPS = 1e-6
L2_EPS = 1e-6
DEEPNORM_ALPHA = (2 * DEPTH) ** 0.25

LANES = 128
HALO = 16
VMEM_LIMIT = 56 * 1024 * 1024
NEG = -1e30
LOG2E = 1.4426950408889634

_OFF = np.cumsum([0, 512, 128, 128, 512, 512, 512, 512, 512, 512, 512, 8, 8, 8, 8])


def _cparams(sem):
    return pltpu.CompilerParams(dimension_semantics=sem, vmem_limit_bytes=VMEM_LIMIT)


def _const_spec(shape):
    nd = len(shape)
    return pl.BlockSpec(shape, lambda *a: (0,) * nd, pipeline_mode=pl.Buffered(1))


def _bdot(a, b):
    return jnp.dot(a.astype(BF16), b.astype(BF16), preferred_element_type=F32)


def _split(x, n):
    out = []
    r = x
    for _ in range(n):
        p = r.astype(BF16)
        out.append(p)
        r = r - p.astype(F32)
    return out


def _dot_exact_rhs(a, b01, n=3):
    acc = None
    for p in _split(a, n):
        t = jnp.dot(p, b01, preferred_element_type=F32)
        acc = t if acc is None else acc + t
    return acc


def _dot_exact_lhs(a01, b, n=3):
    acc = None
    for p in _split(b, n):
        t = jnp.dot(a01, p, preferred_element_type=F32)
        acc = t if acc is None else acc + t
    return acc


def _seg_ones(n, seg):
    r = lax.broadcasted_iota(jnp.int32, (n, n), 0) // seg
    c = lax.broadcasted_iota(jnp.int32, (n, n), 1) // seg
    return jnp.where(r == c, 1.0, 0.0).astype(BF16)


def _seg_sum(x, ones):
    return _dot_exact_rhs(x, ones, 2)


def _layer_norm(y, g, b):
    mu = jnp.mean(y, axis=-1, keepdims=True)
    yc = y - mu
    var = jnp.mean(yc * yc, axis=-1, keepdims=True)
    return yc * lax.rsqrt(var + LN_EPS) * g + b


def _silu(x):
    return x * jax.nn.sigmoid(x)


def _ln_kernel(x_ref, g_ref, b_ref, o_ref):
    o_ref[...] = _layer_norm(x_ref[...], g_ref[...], b_ref[...])


def _ln_call(x, g, b, tm=1024):
    t, d = x.shape
    return pl.pallas_call(
        _ln_kernel,
        grid=(t // tm,),
        in_specs=[pl.BlockSpec((tm, d), lambda i: (i, 0)), _const_spec((1, d)), _const_spec((1, d))],
        out_specs=pl.BlockSpec((tm, d), lambda i: (i, 0)),
        out_shape=jax.ShapeDtypeStruct((t, d), F32),
        compiler_params=_cparams(("parallel",)),
        name="emb_ln",
    )(x, g.reshape(1, d), b.reshape(1, d))


def _proj_kernel(xp_ref, x_ref, xn_ref, cos_ref, sin_ref, wa_ref, wn_ref, wc_ref, wz_ref, wab_ref,
                 qg_ref, kg_ref, cw_ref, gp_ref,
                 qt_ref, ka_ref, vt_ref, nq_ref, nkt_ref, nv_ref, dq_ref, dk_ref, dv_ref, cz_ref, gn_ref,
                 xs_ref, c_ref, *, tm, nts):
    i = pl.program_id(0)
    first = (i % nts) == 0
    last = (i % nts) == nts - 1
    xs_ref[0:HALO, :] = jnp.where(first, 0.0, xp_ref[...]).astype(BF16)
    xs_ref[HALO:HALO + tm, :] = x_ref[...].astype(BF16)
    xs_ref[HALO + tm:, :] = jnp.where(last, 0.0, xn_ref[...]).astype(BF16)
    xm = xs_ref[HALO:HALO + tm, :]

    ones = _seg_ones(LANES, HEAD_DIM)
    cos = cos_ref[...]
    sin = sin_ref[...]
    lane = lax.broadcasted_iota(jnp.int32, (tm, LANES), 1)
    first_half = (lane % (2 * ROPE_PAIRS)) < ROPE_PAIRS

    def norm_rope(xq, g):
        ss = _seg_sum(xq * xq, ones)
        xn = xq * lax.rsqrt(ss * (1.0 / HEAD_DIM) + RMS_EPS) * g
        partner = jnp.where(first_half, pltpu.roll(xn, LANES - ROPE_PAIRS, 1), pltpu.roll(xn, ROPE_PAIRS, 1))
        return xn * cos + partner * sin

    ya = jnp.dot(xm, wa_ref[...], preferred_element_type=F32)
    scale = HEAD_DIM ** -0.5
    scale2 = scale * LOG2E
    for j in range(4):
        sl = slice(LANES * j, LANES * (j + 1))
        qt_ref[0, sl, :] = (norm_rope(ya[:, sl], qg_ref[...]) * scale2).T.astype(BF16)
    ka_ref[...] = norm_rope(ya[:, 512:640], kg_ref[...]).astype(BF16)
    vt = ya[:, 640:768].T.astype(BF16)
    for h in range(ATT_KV_HEADS):
        vt_ref[0, h, 0:HEAD_DIM, :] = vt[HEAD_DIM * h:HEAD_DIM * (h + 1)]
        vt_ref[0, h, HEAD_DIM:, :] = jnp.ones((GQA_ONES_ROWS, tm), BF16)

    yn = jnp.dot(xm, wn_ref[...], preferred_element_type=F32)
    nq_ref[...] = (yn[:, 0:512] * scale2).astype(BF16)
    nkt_ref[0] = yn[:, 512:1024].T.astype(BF16)
    nv_ref[...] = yn[:, 1024:1536].astype(BF16)

    c_ref[...] = jnp.dot(xs_ref[...], wc_ref[...], preferred_element_type=F32)
    for j in range(12):
        sl = slice(LANES * j, LANES * (j + 1))
        y = (c_ref[HALO - 1:HALO - 1 + tm, sl] * cw_ref[0:1, sl]
             + c_ref[HALO:HALO + tm, sl] * cw_ref[1:2, sl]
             + c_ref[HALO + 1:HALO + 1 + tm, sl] * cw_ref[2:3, sl])
        y = _silu(y)
        if j < 8:
            y = y * lax.rsqrt(_seg_sum(y * y, ones) + L2_EPS)
        if j < 4:
            y = y * scale
        osl = slice(LANES * (j % 4), LANES * (j % 4 + 1))
        (dq_ref, dk_ref, dv_ref)[j // 4][:, osl] = y
    cz_ref[...] = jnp.dot(xm, wz_ref[...], preferred_element_type=F32)

    raw = jnp.dot(xm, wab_ref[...], preferred_element_type=F32)
    a_log = gp_ref[0:1, :]
    dt_b = gp_ref[1:2, :]
    is_g = gp_ref[2:3, :] > 0.5
    z = raw + dt_b
    softplus = jnp.maximum(z, 0.0) + jnp.log(1.0 + jnp.exp(-jnp.abs(z)))
    gn_ref[...] = jnp.where(is_g, -jnp.exp(a_log) * softplus, jax.nn.sigmoid(raw))


def _proj_call(x, bsz, seq, cos_t, sin_t, wa, wn, wc, wz, wab, qg, kg, cw, gp, tm=512):
    t = x.shape[0]
    nts = seq // tm
    nh = t // HALO
    row = lambda i: (i, 0)
    seqrow = lambda i: (i % nts, 0)
    in_specs = [
        pl.BlockSpec((HALO, D_MODEL), lambda i: (jnp.maximum(i * (tm // HALO) - 1, 0), 0)),
        pl.BlockSpec((tm, D_MODEL), row),
        pl.BlockSpec((HALO, D_MODEL), lambda i: (jnp.minimum((i + 1) * (tm // HALO), nh - 1), 0)),
        pl.BlockSpec((tm, LANES), seqrow),
        pl.BlockSpec((tm, LANES), seqrow),
        _const_spec(wa.shape), _const_spec(wn.shape), _const_spec(wc.shape), _const_spec(wz.shape),
        _const_spec(wab.shape), _const_spec(qg.shape), _const_spec(kg.shape), _const_spec(cw.shape),
        _const_spec(gp.shape),
    ]
    tposed = lambda i: (i // nts, 0, i % nts)
    out_specs = [
        pl.BlockSpec((1, 512, tm), tposed),
        pl.BlockSpec((tm, LANES), row),
        pl.BlockSpec((1, ATT_KV_HEADS, HEAD_DIM + GQA_ONES_ROWS, tm), lambda i: (i // nts, 0, 0, i % nts)),
        pl.BlockSpec((tm, 512), row),
        pl.BlockSpec((1, 512, tm), tposed),
        pl.BlockSpec((tm, 512), row),
        pl.BlockSpec((tm, 512), row),
        pl.BlockSpec((tm, 512), row),
        pl.BlockSpec((tm, 512), row),
        pl.BlockSpec((tm, 512), row),
        pl.BlockSpec((tm, LANES), row),
    ]
    out_shape = [
        jax.ShapeDtypeStruct((bsz, 512, seq), BF16),
        jax.ShapeDtypeStruct((t, LANES), BF16),
        jax.ShapeDtypeStruct((bsz, ATT_KV_HEADS, HEAD_DIM + GQA_ONES_ROWS, seq), BF16),
        jax.ShapeDtypeStruct((t, 512), BF16),
        jax.ShapeDtypeStruct((bsz, 512, seq), BF16),
        jax.ShapeDtypeStruct((t, 512), BF16),
        jax.ShapeDtypeStruct((t, 512), F32),
        jax.ShapeDtypeStruct((t, 512), F32),
        jax.ShapeDtypeStruct((t, 512), F32),
        jax.ShapeDtypeStruct((t, 512), F32),
        jax.ShapeDtypeStruct((t, LANES), F32),
    ]
    return pl.pallas_call(
        functools.partial(_proj_kernel, tm=tm, nts=nts),
        grid=(t // tm,),
        in_specs=in_specs,
        out_specs=out_specs,
        out_shape=out_shape,
        scratch_shapes=[pltpu.VMEM((tm + 2 * HALO, D_MODEL), BF16), pltpu.VMEM((tm + 2 * HALO, 1536), F32)],
        compiler_params=_cparams(("parallel",)),
        name="proj",
    )(x, x, x, cos_t, sin_t, wa, wn, wc, wz, wab, qg, kg, cw, gp)


GQA_TQ = 128
GQA_ONES_ROWS = 16
GQA_KEY_CHUNK = 1024


def _gqa_kernel(qt_ref, k_ref, vt_ref, o_ref):
    seq = k_ref.shape[1]
    kc = GQA_KEY_CHUNK
    nkc = seq // kc
    zeros = jnp.zeros((HEAD_DIM, 2 * GQA_TQ), BF16)
    groups = [(h, 4 * h + 2 * pr) for h in range(ATT_KV_HEADS) for pr in range(2)]

    def scores(h, g0):
        w = jnp.concatenate([qt_ref[0, HEAD_DIM * g0:HEAD_DIM * (g0 + 1), :],
                             qt_ref[0, HEAD_DIM * (g0 + 1):HEAD_DIM * (g0 + 2), :]], axis=1)
        w = jnp.concatenate([w, zeros] if h == 0 else [zeros, w], axis=0)
        sts = [jnp.dot(k_ref[0, kc * c:kc * (c + 1), :], w, preferred_element_type=F32) for c in range(nkc)]
        m = functools.reduce(jnp.maximum, [jnp.max(st, axis=0, keepdims=True) for st in sts])
        return sts, m

    def outputs(h, g0, sts, m):
        ot = None
        for c in range(nkc):
            pt = jnp.exp2(sts[c] - m).astype(BF16)
            part = jnp.dot(vt_ref[0, h, :, kc * c:kc * (c + 1)], pt, preferred_element_type=F32)
            ot = part if ot is None else ot + part
        on = ot[:HEAD_DIM] * (1.0 / ot[HEAD_DIM:HEAD_DIM + 1])
        both = jnp.concatenate([on[:, :GQA_TQ], on[:, GQA_TQ:]], axis=0)
        o_ref[0, :, HEAD_DIM * g0:HEAD_DIM * (g0 + 2)] = both.T.astype(BF16)

    pending = scores(*groups[0])
    for n, grp in enumerate(groups):
        nxt = scores(*groups[n + 1]) if n + 1 < len(groups) else None
        outputs(*grp, *pending)
        pending = nxt


def _gqa_call(qt, ka, vt):
    bsz, _, seq = qt.shape
    return pl.pallas_call(
        _gqa_kernel,
        grid=(bsz, seq // GQA_TQ),
        in_specs=[
            pl.BlockSpec((1, 512, GQA_TQ), lambda b, i: (b, 0, i)),
            pl.BlockSpec((1, seq, LANES), lambda b, i: (b, 0, 0)),
            pl.BlockSpec((1,) + vt.shape[1:], lambda b, i: (b, 0, 0, 0)),
        ],
        out_specs=pl.BlockSpec((1, GQA_TQ, 512), lambda b, i: (b, i, 0)),
        out_shape=jax.ShapeDtypeStruct((bsz, seq, 512), BF16),
        compiler_params=_cparams(("parallel", "arbitrary")),
        name="gqa",
    )(qt, ka, vt)


NA_QBLK = 128


def _na_geometry(seq):
    rows = seq // GRID_W
    wr = min(NA_WIN_ROWS, rows)
    wc = NA_WIN_COLS
    qrows = NA_QBLK // GRID_W
    nblk = seq // NA_QBLK
    nkb = min(5, nblk)
    krows = nkb * qrows
    starts = np.clip(np.arange(nblk) - 2, 0, nblk - nkb)
    variants, var_of = [], []
    for i in range(nblk):
        r = i * qrows + np.arange(qrows)
        r0 = np.clip(r - wr // 2, 0, rows - wr)
        kr = starts[i] * qrows + np.arange(krows)
        valid = (kr[None] >= r0[:, None]) & (kr[None] < r0[:, None] + wr)
        assert (valid.sum(1) == wr).all()
        dr = kr[None] - r[:, None] + (NA_WIN_ROWS - 1)
        sel = (valid[..., None] & (dr[..., None] == np.arange(2 * NA_WIN_ROWS - 1))).astype(np.float32)
        key = sel.tobytes()
        if key not in variants:
            variants.append(key)
        var_of.append(variants.index(key))
    row_sel = np.stack([np.frombuffer(k, np.float32).reshape(qrows, krows, 2 * NA_WIN_ROWS - 1) for k in variants])
    c = np.arange(GRID_W)
    c0 = np.clip(c - wc // 2, 0, GRID_W - wc)
    validc = (c[None] >= c0[:, None]) & (c[None] < c0[:, None] + wc)
    dc = c[None] - c[:, None] + (NA_WIN_COLS - 1)
    col_sel = (validc[..., None] & (dc[..., None] == np.arange(2 * NA_WIN_COLS - 1))).astype(np.float32)
    return nkb, starts.astype(np.int32), np.asarray(var_of, np.int32), row_sel, col_sel


def _na_bias_table(rpb, row_sel, col_sel):
    hi = lax.Precision.HIGHEST
    cols = jnp.einsum('hrd,cnd->hrcn', rpb, col_sel, precision=hi)
    tab = jnp.einsum('vqkr,hrcn->vhqckn', row_sel, cols, precision=hi)
    valid = (row_sel.sum(-1) > 0)[:, None, :, None, :, None] & (col_sel.sum(-1) > 0)[None, None, None, :, None, :]
    tab = jnp.where(valid, tab * LOG2E, NEG)
    nv, h = tab.shape[:2]
    return tab.reshape(nv, h // 2, 2 * NA_QBLK, -1)


def _na_kernel(var_ref, kb_ref, q_ref, *refs, nkb):
    kt_refs = refs[:nkb]
    v_refs = refs[nkb:2 * nkb]
    bias_ref = refs[2 * nkb]
    o_ref = refs[2 * nkb + 1]
    lane = lax.broadcasted_iota(jnp.int32, (NA_QBLK, LANES), 1)
    lo = lane < HEAD_DIM
    for pp in range(NA_HEADS // 2):
        sl = slice(LANES * pp, LANES * (pp + 1))
        qp = q_ref[0, :, sl]
        zero = jnp.zeros_like(qp)
        q2 = jnp.concatenate([jnp.where(lo, qp, zero), jnp.where(lo, zero, qp)], axis=0)
        ktp = jnp.concatenate([r[0, sl, :] for r in kt_refs], axis=1)
        vp = jnp.concatenate([r[0, :, sl] for r in v_refs], axis=0)
        s = jnp.dot(q2, ktp, preferred_element_type=F32) + bias_ref[0, pp]
        m = jnp.max(s, axis=-1, keepdims=True)
        p = jnp.exp2(s - m)
        l = jnp.sum(p, axis=-1, keepdims=True)
        pv = jnp.dot(p.astype(BF16), vp, preferred_element_type=F32) / l
        o_ref[0, :, sl] = jnp.where(lo, pv[:NA_QBLK], pv[NA_QBLK:]).astype(BF16)


def _na_call(nq, nkt, nv, rpb):
    bsz, seq, _ = nq.shape
    nkb, starts, var_of, row_sel, col_sel = _na_geometry(seq)
    bias = _na_bias_table(rpb.astype(F32), row_sel, col_sel)
    nblk = seq // NA_QBLK
    in_specs = [pl.BlockSpec((1, NA_QBLK, 512), lambda b, i, var, kb: (b, i, 0))]
    for j in range(nkb):
        in_specs.append(pl.BlockSpec((1, 512, NA_QBLK), lambda b, i, var, kb, j=j: (b, 0, kb[i] + j)))
    for j in range(nkb):
        in_specs.append(pl.BlockSpec((1, NA_QBLK, 512), lambda b, i, var, kb, j=j: (b, kb[i] + j, 0)))
    in_specs.append(pl.BlockSpec((1,) + bias.shape[1:], lambda b, i, var, kb: (var[i], 0, 0, 0)))
    grid_spec = pltpu.PrefetchScalarGridSpec(
        num_scalar_prefetch=2,
        grid=(bsz, nblk),
        in_specs=in_specs,
        out_specs=pl.BlockSpec((1, NA_QBLK, 512), lambda b, i, var, kb: (b, i, 0)),
    )
    return pl.pallas_call(
        functools.partial(_na_kernel, nkb=nkb),
        grid_spec=grid_spec,
        out_shape=jax.ShapeDtypeStruct((bsz, seq, 512), BF16),
        compiler_params=_cparams(("parallel", "arbitrary")),
        name="na",
    )(jnp.asarray(var_of), jnp.asarray(starts), nq, *([nkt] * nkb), *([nv] * nkb), bias)


DN_TC = 512
DN_NHP = 2


def _bd(x, lo):
    xb = x.astype(BF16)
    zero = jnp.zeros_like(xb)
    return jnp.concatenate([jnp.where(lo, xb, zero), jnp.where(lo, zero, xb)], axis=-2)


def _bmm(a, b16):
    return jnp.einsum('bik,bkj->bij', a.astype(BF16), b16, preferred_element_type=F32)


def _dn_prepare(d, hp, q, k, v, gn, gr):
    C = DN_CHUNK
    nch = DN_TC // C
    sgn = 1 - 2 * d

    col = lax.broadcasted_iota(jnp.int32, (LANES, 2 * LANES), 0)
    ln2 = lax.broadcasted_iota(jnp.int32, (LANES, 2 * LANES), 1)
    want = (ln2 // LANES) * 16 + d * 8 + 2 * hp + (ln2 % LANES) // HEAD_DIM
    onehot = jnp.where(col == want, 1.0, 0.0).astype(BF16)
    gb = _dot_exact_rhs(gn, onehot)
    g_col = gb[:, :LANES]
    beta = gb[:, LANES:]
    pos = lax.broadcasted_iota(jnp.int32, (DN_TC, LANES), 0) % C
    gc = g_col
    shift = 1
    while shift < C:
        if d == 0:
            gc = gc + jnp.where(pos >= shift, pltpu.roll(gc, shift, 0), 0.0)
        else:
            gc = gc + jnp.where(pos < C - shift, pltpu.roll(gc, DN_TC - shift, 0), 0.0)
        shift *= 2
    last = C - 1 if d == 0 else 0
    tot = jnp.concatenate([jnp.broadcast_to(gc[C * j + last:C * j + last + 1, :], (C, LANES)) for j in range(nch)], axis=0)
    r2 = lax.broadcasted_iota(jnp.int32, (LANES, LANES), 0)
    c2 = lax.broadcasted_iota(jnp.int32, (LANES, LANES), 1)
    tri2 = jnp.where((r2 // C == c2 // C) & ((c2 - r2) * sgn >= 0), 1.0, 0.0).astype(BF16)
    gcr = _dot_exact_rhs(gr, tri2)

    egc = jnp.exp(gc)
    kbeta = k * beta
    vbeta = v * beta
    kbd = kbeta * egc
    qd = q * egc
    kd = k * jnp.exp(tot - gc)
    etot = jnp.exp(tot)

    row = lax.broadcasted_iota(jnp.int32, (C, LANES), 0)
    jj = lax.broadcasted_iota(jnp.int32, (C, LANES), 1) % HEAD_DIM
    earlier_eq = (row - jj) * sgn >= 0
    strict = jnp.where((row - jj) * sgn > 0, 1.0, 0.0)

    def chunks(a):
        return a.reshape(nch, C, a.shape[-1])

    gc3 = chunks(gc)
    diff = jnp.stack([gc3[j] - gcr[j:j + 1, :] for j in range(nch)])
    decay = jnp.where(earlier_eq, jnp.exp(jnp.minimum(diff, 0.0)), 0.0)
    return dict(q=chunks(q), k=chunks(k), kbeta=chunks(kbeta), vbeta=chunks(vbeta), kbd=chunks(kbd), qd=chunks(qd),
                kd=chunks(kd), etot=chunks(etot), decay=decay, decay_strict=decay * strict)


def _dn_solve(e):
    C = DN_CHUNK
    row = lax.broadcasted_iota(jnp.int32, (C, LANES), 0)
    lane = lax.broadcasted_iota(jnp.int32, (C, LANES), 1)
    lo = lane < HEAD_DIM
    eye2 = jnp.where(row == lane % HEAD_DIM, 1.0, 0.0)
    ai = jnp.einsum('bik,bjk->bij', jnp.concatenate([e['kbeta'], e['q']], axis=1).astype(BF16), _bd(e['k'], lo),
                    preferred_element_type=F32)
    intra = ai[:, C:] * e['decay']
    x = -ai[:, :C] * e['decay_strict']
    t = eye2 + x
    p = _bmm(x, _bd(x, lo))
    for lvl in range(5):
        pbd = _bd(p, lo)
        if lvl < 4:
            rr = _bmm(jnp.concatenate([p, t], axis=1), pbd)
            p = rr[:, :C]
            t = t + rr[:, C:]
        else:
            t = t + _bmm(t, pbd)
    uw = _bmm(t, jnp.concatenate([_bd(e['vbeta'], lo), _bd(e['kbd'], lo)], axis=2))
    kdt = jnp.stack([e['kd'][b].T for b in range(e['kd'].shape[0])]).astype(BF16)
    r3 = lax.broadcasted_iota(jnp.int32, (LANES, 2 * LANES), 0) // HEAD_DIM
    c3 = (lax.broadcasted_iota(jnp.int32, (LANES, 2 * LANES), 1) % LANES) // HEAD_DIM
    nm = jnp.where(r3 == c3, _bmm(kdt, uw.astype(BF16)), 0.0)
    return uw, intra, nm


def _dn_kernel(qf_ref, kf_ref, vf_ref, gnf_ref, grf_ref, qb_ref, kb_ref, vb_ref, gnb_ref, grb_ref,
               of_ref, ob_ref, s_ref):
    C = DN_CHUNK
    nch = DN_TC // C

    @pl.when(pl.program_id(2) == 0)
    def _():
        s_ref[...] = jnp.zeros_like(s_ref)

    in_refs = ((qf_ref, kf_ref, vf_ref, gnf_ref, grf_ref), (qb_ref, kb_ref, vb_ref, gnb_ref, grb_ref))
    out_refs = (of_ref, ob_ref)
    lo = lax.broadcasted_iota(jnp.int32, (C, LANES), 1) < HEAD_DIM

    chains = [(d, i) for d in range(2) for i in range(DN_NHP)]
    per = []
    for d, i in chains:
        q_ref, k_ref, v_ref, gn_ref, gr_ref = in_refs[d]
        sl = slice(LANES * i, LANES * (i + 1))
        hp = pl.program_id(1) * DN_NHP + i
        per.append(_dn_prepare(d, hp, q_ref[0, :, sl], k_ref[0, :, sl], v_ref[0, :, sl], gn_ref[0], gr_ref[0, i, 0]))
    e = {name: jnp.concatenate([p[name] for p in per], axis=0) for name in per[0]}
    uw, intra, nm = _dn_solve(e)
    qd, etot = e['qd'], e['etot']
    states = [s_ref[d, i] for d, i in chains]
    for step in range(nch):
        for n, (d, i) in enumerate(chains):
            j = step if d == 0 else nch - 1 - step
            b = n * nch + j
            s16 = states[n].astype(BF16)
            states[n] = (states[n] * etot[b, 0:1] + nm[b, :, :LANES]
                         - jnp.dot(nm[b, :, LANES:].astype(BF16), s16, preferred_element_type=F32))
            wq = jnp.dot(jnp.concatenate([uw[b, :, LANES:], qd[b]], axis=0).astype(BF16), s16,
                         preferred_element_type=F32)
            v_new = uw[b, :, :LANES] - wq[:C]
            out_refs[d][0, C * j:C * (j + 1), LANES * i:LANES * (i + 1)] = wq[C:] + jnp.dot(
                intra[b].astype(BF16), _bd(v_new, lo), preferred_element_type=F32)
    for n, (d, i) in enumerate(chains):
        s_ref[d, i] = states[n]


def _dn_call(dq, dk, dv, gn, gr):
    bsz, seq, _ = dq.shape
    nsc = seq // DN_TC
    nch = DN_TC // DN_CHUNK
    width = LANES * DN_NHP
    in_specs = []
    for d in range(2):
        tmap = (lambda c: c) if d == 0 else (lambda c: nsc - 1 - c)
        qkv_spec = pl.BlockSpec((1, DN_TC, width), lambda b, hp, c, tmap=tmap: (b, tmap(c), hp))
        in_specs += [
            qkv_spec, qkv_spec, qkv_spec,
            pl.BlockSpec((1, DN_TC, LANES), lambda b, hp, c, tmap=tmap: (b, tmap(c), 0)),
            pl.BlockSpec((1, DN_NHP, 1, nch, LANES), lambda b, hp, c, tmap=tmap, d=d: (d, hp, b, tmap(c), 0)),
        ]
    out_specs = [
        pl.BlockSpec((1, DN_TC, width), lambda b, hp, c: (b, c, hp)),
        pl.BlockSpec((1, DN_TC, width), lambda b, hp, c: (b, nsc - 1 - c, hp)),
    ]
    return pl.pallas_call(
        _dn_kernel,
        grid=(bsz, DN_HEADS // 2 // DN_NHP, nsc),
        in_specs=in_specs,
        out_specs=out_specs,
        out_shape=[jax.ShapeDtypeStruct((bsz, seq, 512), F32)] * 2,
        scratch_shapes=[pltpu.VMEM((2, DN_NHP, LANES, LANES), F32)],
        compiler_params=_cparams(("parallel", "parallel", "arbitrary")),
        name="deltanet",
    )(dq, dk, dv, gn, gr, dq, dk, dv, gn, gr)


def _dn_rows(gn, bsz, seq):
    g = gn[:, :32].reshape(bsz, seq // DN_CHUNK, DN_CHUNK, 2, 2, DN_HEADS // 2, 2)
    g = g[:, :, :, 0]
    g = jnp.transpose(g, (3, 4, 0, 1, 5, 2))
    return g.reshape(2, DN_HEADS // 2, bsz, seq // DN_CHUNK, LANES)


def _merge_kernel(x_ref, bra_ref, brb_ref, of_ref, ob_ref, cz_ref, wg_ref, bg_ref, wb_ref, wo_ref,
                  ng_ref, g1_ref, b1_ref, o_ref):
    x = x_ref[...]
    xb = x.astype(BF16)
    ones = _seg_ones(LANES, HEAD_DIM)
    parts = []
    for j in range(4):
        sl = slice(LANES * j, LANES * (j + 1))
        o = of_ref[:, sl] + ob_ref[:, sl]
        ss = _seg_sum(o * o, ones)
        parts.append((o * lax.rsqrt(ss * (1.0 / HEAD_DIM) + RMS_EPS) * ng_ref[...] * _silu(cz_ref[:, sl])).astype(BF16))
    brc = jnp.concatenate(parts, axis=1)
    merged = None
    for n, br in enumerate((bra_ref[...], brb_ref[...], brc)):
        sl = slice(D_MODEL * n, D_MODEL * (n + 1))
        gate = jax.nn.sigmoid(jnp.dot(xb, wg_ref[:, sl], preferred_element_type=F32) + bg_ref[:, sl])
        term = gate * jnp.dot(br, wb_ref[n], preferred_element_type=F32)
        merged = term if merged is None else merged + term
    y = DEEPNORM_ALPHA * x + jnp.dot(merged.astype(BF16), wo_ref[...], preferred_element_type=F32)
    o_ref[...] = _layer_norm(y, g1_ref[...], b1_ref[...])


def _merge_call(x, bra, brb, o_f, o_b, cz, wg, bg, wb, wo, ng, g1, b1, tm=512):
    t = x.shape[0]
    row = lambda i: (i, 0)
    return pl.pallas_call(
        _merge_kernel,
        grid=(t // tm,),
        in_specs=[
            pl.BlockSpec((tm, D_MODEL), row),
            pl.BlockSpec((tm, 512), row),
            pl.BlockSpec((tm, 512), row),
            pl.BlockSpec((tm, 512), row),
            pl.BlockSpec((tm, 512), row),
            pl.BlockSpec((tm, 512), row),
            _const_spec(wg.shape), _const_spec(bg.shape), _const_spec(wb.shape), _const_spec(wo.shape),
            _const_spec(ng.shape), _const_spec(g1.shape), _const_spec(b1.shape),
        ],
        out_specs=pl.BlockSpec((tm, D_MODEL), row),
        out_shape=jax.ShapeDtypeStruct((t, D_MODEL), F32),
        compiler_params=_cparams(("parallel",)),
        name="merge",
    )(x, bra, brb, o_f, o_b, cz, wg, bg, wb, wo, ng, g1, b1)


def _ffn_kernel(x_ref, p_ref, wfg_ref, wfu_ref, wfo_ref, wpg_ref, bpg_ref, wpp_ref, g2_ref, b2_ref, o_ref):
    x = x_ref[...]
    xb = x.astype(BF16)
    gate = jnp.dot(xb, wfg_ref[...], preferred_element_type=F32)
    up = jnp.dot(xb, wfu_ref[...], preferred_element_type=F32)
    ffn = jnp.dot((_silu(gate) * up).astype(BF16), wfo_ref[...], preferred_element_type=F32)
    ple = (jax.nn.sigmoid(jnp.dot(xb, wpg_ref[...], preferred_element_type=F32) + bpg_ref[...])
           * jnp.dot(p_ref[...].astype(BF16), wpp_ref[...], preferred_element_type=F32))
    o_ref[...] = _layer_norm(DEEPNORM_ALPHA * x + ffn + ple, g2_ref[...], b2_ref[...])


def _ffn_call(x, p, wfg, wfu, wfo, wpg, bpg, wpp, g2, b2, tm=256):
    t = x.shape[0]
    row = lambda i: (i, 0)
    return pl.pallas_call(
        _ffn_kernel,
        grid=(t // tm,),
        in_specs=[
            pl.BlockSpec((tm, D_MODEL), row),
            pl.BlockSpec((tm, PLE_DIM), row),
            _const_spec(wfg.shape), _const_spec(wfu.shape), _const_spec(wfo.shape), _const_spec(wpg.shape),
            _const_spec(bpg.shape), _const_spec(wpp.shape), _const_spec(g2.shape), _const_spec(b2.shape),
        ],
        out_specs=pl.BlockSpec((tm, D_MODEL), row),
        out_shape=jax.ShapeDtypeStruct((t, D_MODEL), F32),
        compiler_params=_cparams(("parallel",)),
        name="ffn",
    )(x, p, wfg, wfu, wfo, wpg, bpg, wpp, g2, b2)


def _rope_tables(seq):
    t = jnp.arange(seq, dtype=jnp.int32)
    rowf = (t // GRID_W).astype(F32)
    colf = (t % GRID_W).astype(F32)
    inv_freq = ROPE_THETA ** (-jnp.arange(ROPE_PAIRS, dtype=F32) / ROPE_PAIRS)
    ang_r = rowf[:, None] * inv_freq[None, :]
    ang_c = colf[:, None] * inv_freq[None, :]
    cos = jnp.concatenate([jnp.cos(ang_r), jnp.cos(ang_r), jnp.cos(ang_c), jnp.cos(ang_c)], axis=1)
    sin = jnp.concatenate([-jnp.sin(ang_r), jnp.sin(ang_r), -jnp.sin(ang_c), jnp.sin(ang_c)], axis=1)
    return jnp.tile(cos, (1, 2)), jnp.tile(sin, (1, 2))


def _tile_lanes(v, reps):
    return jnp.tile(v.astype(F32), reps).reshape(1, -1)


def _layer(x, p_i, bsz, seq, tables, w):
    (w_in, qn_g, kn_g, rpb, conv_w, a_log, dt_bias, dn_g, w_gate, b_gate, w_branch, w_out, ln1_g, ln1_b,
     w_ffn_in, w_ffn_out, w_ple_gate, b_ple_gate, w_ple_proj, ln2_g, ln2_b) = w
    o = _OFF
    wa = w_in[:, o[0]:o[3]].astype(BF16)
    wn = w_in[:, o[3]:o[6]].astype(BF16)
    wc = w_in[:, o[6]:o[9]].astype(BF16)
    wz = w_in[:, o[9]:o[10]].astype(BF16)
    wab = jnp.pad(w_in[:, o[10]:o[14]], ((0, 0), (0, LANES - 32))).astype(BF16)
    gp = jnp.zeros((8, LANES), F32)
    gp = gp.at[0, :16].set(a_log.reshape(16).astype(F32))
    gp = gp.at[1, :16].set(dt_bias.reshape(16).astype(F32))
    gp = gp.at[2, :16].set(1.0)
    cos_t, sin_t = tables
    (qt, ka, vt, nq, nkt, nv, dq, dk, dv, cz, gn) = _proj_call(
        x, bsz, seq, cos_t, sin_t, wa, wn, wc, wz, wab,
        _tile_lanes(qn_g, 2), _tile_lanes(kn_g, 2), conv_w.astype(F32), gp)
    r3 = lambda a: a.reshape(bsz, seq, a.shape[-1])
    bra = _gqa_call(qt, r3(ka), vt)
    brb = _na_call(r3(nq), nkt, r3(nv), rpb)
    o_f, o_b = _dn_call(r3(dq), r3(dk), r3(dv), r3(gn), _dn_rows(gn, bsz, seq))
    t = bsz * seq
    x = _merge_call(x, bra.reshape(t, 512), brb.reshape(t, 512), o_f.reshape(t, 512), o_b.reshape(t, 512), cz,
                    w_gate.astype(BF16), b_gate.astype(F32).reshape(1, -1), w_branch.astype(BF16),
                    w_out.astype(BF16), _tile_lanes(dn_g, 2), ln1_g.astype(F32).reshape(1, -1),
                    ln1_b.astype(F32).reshape(1, -1))
    x = _ffn_call(x, p_i, w_ffn_in[:, :FFN_HIDDEN].astype(BF16), w_ffn_in[:, FFN_HIDDEN:].astype(BF16),
                  w_ffn_out.astype(BF16), w_ple_gate.astype(BF16), b_ple_gate.astype(F32).reshape(1, -1),
                  w_ple_proj.astype(BF16), ln2_g.astype(F32).reshape(1, -1), ln2_b.astype(F32).reshape(1, -1))
    return x


def _trunk(x, p, emb_ln_g, emb_ln_b, weights):
    bsz, seq, d = x.shape
    depth = p.shape[0]
    t = bsz * seq
    xf = _ln_call(x.reshape(t, d), emb_ln_g.astype(F32), emb_ln_b.astype(F32))
    tables = _rope_tables(seq)
    for i in range(depth):
        xf = _layer(xf, p[i].reshape(t, -1), bsz, seq, tables, tuple(a[i] for a in weights))
    return xf.reshape(bsz, seq, d)


def kernel(x_prompt, x_sample, p_prompt, p_sample, emb_ln_g, emb_ln_b, w_in, att_q_norm_g, att_k_norm_g, na_rpb, dn_conv_w, dn_a_log, dn_dt_bias, dn_norm_g, w_gate, b_gate, w_branch, w_out, ln1_g, ln1_b, w_ffn_in, w_ffn_out, w_ple_gate, b_ple_gate, w_ple_proj, ln2_g, ln2_b):
    weights = (w_in, att_q_norm_g, att_k_norm_g, na_rpb, dn_conv_w, dn_a_log, dn_dt_bias, dn_norm_g, w_gate,
               b_gate, w_branch, w_out, ln1_g, ln1_b, w_ffn_in, w_ffn_out, w_ple_gate, b_ple_gate, w_ple_proj,
               ln2_g, ln2_b)
    nb = x_prompt.shape[0]
    x = jnp.concatenate([x_prompt, x_sample], axis=0)
    p = jnp.concatenate([p_prompt, p_sample], axis=1)
    y = _trunk(x, p, emb_ln_g, emb_ln_b, weights)
    return (y[:nb], y[nb:])
```

```python
import functools

import numpy as np
import jax
import jax.numpy as jnp
from jax import lax
from jax.experimental import pallas as pl
from jax.experimental.pallas import tpu as pltpu

F32 = jnp.float32
BF16 = jnp.bfloat16

D_MODEL = 1024
DEPTH = 4
GRID_W = 64
HEAD_DIM = 64
ATT_Q_HEADS = 8
ATT_KV_HEADS = 2
ROPE_PAIRS = HEAD_DIM // 4
ROPE_THETA = 10000.0
NA_HEADS = 8
NA_WIN_ROWS = 8
NA_WIN_COLS = 16
DN_HEADS = 8
DN_CHUNK = 64
BRANCH_WIDTH = 512
FFN_HIDDEN = 2816
PLE_DIM = 256
LN_EPS = 1e-5
RMS_EPS = 1e-6
L2_EPS = 1e-6
DEEPNORM_ALPHA = (2 * DEPTH) ** 0.25

LANES = 128
HALO = 16
VMEM_LIMIT = 56 * 1024 * 1024
NEG = -1e30
LOG2E = 1.4426950408889634

_OFF = np.cumsum([0, 512, 128, 128, 512, 512, 512, 512, 512, 512, 512, 8, 8, 8, 8])


def _cparams(sem):
    return pltpu.CompilerParams(dimension_semantics=sem, vmem_limit_bytes=VMEM_LIMIT)


def _const_spec(shape):
    nd = len(shape)
    return pl.BlockSpec(shape, lambda *a: (0,) * nd, pipeline_mode=pl.Buffered(1))


def _bdot(a, b):
    return jnp.dot(a.astype(BF16), b.astype(BF16), preferred_element_type=F32)


def _split(x, n):
    out = []
    r = x
    for _ in range(n):
        p = r.astype(BF16)
        out.append(p)
        r = r - p.astype(F32)
    return out


def _dot_exact_rhs(a, b01, n=3):
    acc = None
    for p in _split(a, n):
        t = jnp.dot(p, b01, preferred_element_type=F32)
        acc = t if acc is None else acc + t
    return acc


def _dot_exact_lhs(a01, b, n=3):
    acc = None
    for p in _split(b, n):
        t = jnp.dot(a01, p, preferred_element_type=F32)
        acc = t if acc is None else acc + t
    return acc


def _seg_ones(n, seg):
    r = lax.broadcasted_iota(jnp.int32, (n, n), 0) // seg
    c = lax.broadcasted_iota(jnp.int32, (n, n), 1) // seg
    return jnp.where(r == c, 1.0, 0.0).astype(BF16)


def _seg_sum(x, ones):
    return _dot_exact_rhs(x, ones, 1)


def _layer_norm(y, g, b):
    mu = jnp.mean(y, axis=-1, keepdims=True)
    yc = y - mu
    var = jnp.mean(yc * yc, axis=-1, keepdims=True)
    return yc * lax.rsqrt(var + LN_EPS) * g + b


def _silu(x):
    return x * jax.nn.sigmoid(x)


def _two_group_specs(block, n_first):
    nd = len(block)
    tail = (0,) * (nd - 1)
    return [pl.BlockSpec(block, lambda i: (jnp.minimum(i, n_first - 1),) + tail),
            pl.BlockSpec(block, lambda i: (jnp.maximum(i - n_first, 0),) + tail)]


def _ln_kernel(xa_ref, xb_ref, g_ref, b_ref, o_ref, *, n_first):
    x = jnp.where(pl.program_id(0) < n_first, xa_ref[...], xb_ref[...])
    o_ref[...] = _layer_norm(x, g_ref[...], b_ref[...])


def _ln_call(xa, xb, g, b, tm=1024):
    d = xa.shape[1]
    t = xa.shape[0] + xb.shape[0]
    n_first = xa.shape[0] // tm
    return pl.pallas_call(
        functools.partial(_ln_kernel, n_first=n_first),
        grid=(t // tm,),
        in_specs=_two_group_specs((tm, d), n_first) + [_const_spec((1, d)), _const_spec((1, d))],
        out_specs=pl.BlockSpec((tm, d), lambda i: (i, 0)),
        out_shape=jax.ShapeDtypeStruct((t, d), F32),
        compiler_params=_cparams(("arbitrary",)),
        name="emb_ln",
    )(xa, xb, g.reshape(1, d), b.reshape(1, d))


def _proj_kernel(xp_ref, x_ref, xn_ref, cos_ref, sin_ref, wa_ref, wn_ref, wc_ref, wz_ref, wab_ref,
                 qg_ref, kg_ref, cw_ref, gp_ref,
                 qt_ref, ka_ref, vt_ref, nq_ref, nkt_ref, nv_ref, dq_ref, dk_ref, dv_ref, cz_ref, gn_ref,
                 xs_ref, c_ref, *, tm, nts):
    i = pl.program_id(0)
    first = (i % nts) == 0
    last = (i % nts) == nts - 1
    xs_ref[0:HALO, :] = jnp.where(first, 0.0, xp_ref[...]).astype(BF16)
    xs_ref[HALO:HALO + tm, :] = x_ref[...].astype(BF16)
    xs_ref[HALO + tm:, :] = jnp.where(last, 0.0, xn_ref[...]).astype(BF16)
    xm = xs_ref[HALO:HALO + tm, :]

    ones = _seg_ones(LANES, HEAD_DIM)
    cos = cos_ref[...]
    sin = sin_ref[...]
    lane = lax.broadcasted_iota(jnp.int32, (tm, LANES), 1)
    first_half = (lane % (2 * ROPE_PAIRS)) < ROPE_PAIRS

    def norm_rope(xq, g):
        ss = _seg_sum(xq * xq, ones)
        xn = xq * lax.rsqrt(ss * (1.0 / HEAD_DIM) + RMS_EPS) * g
        partner = jnp.where(first_half, pltpu.roll(xn, LANES - ROPE_PAIRS, 1), pltpu.roll(xn, ROPE_PAIRS, 1))
        return xn * cos + partner * sin

    scale = HEAD_DIM ** -0.5
    scale2 = scale * LOG2E

    c_ref[...] = jnp.dot(xs_ref[...], wc_ref[...], preferred_element_type=F32)

    def conv_group(j):
        sl = slice(LANES * j, LANES * (j + 1))
        y = (c_ref[HALO - 1:HALO - 1 + tm, sl] * cw_ref[0:1, sl]
             + c_ref[HALO:HALO + tm, sl] * cw_ref[1:2, sl]
             + c_ref[HALO + 1:HALO + 1 + tm, sl] * cw_ref[2:3, sl])
        y = _silu(y)
        if j < 8:
            y = y * lax.rsqrt(_seg_sum(y * y, ones) + L2_EPS)
        if j < 4:
            y = y * scale
        osl = slice(LANES * (j % 4), LANES * (j % 4 + 1))
        (dq_ref, dk_ref, dv_ref)[j // 4][:, osl] = y

    ya = jnp.dot(xm, wa_ref[...], preferred_element_type=F32)
    for j in range(0, 4):
        conv_group(j)
    for j in range(4):
        sl = slice(LANES * j, LANES * (j + 1))
        qt_ref[0, sl, :] = (norm_rope(ya[:, sl], qg_ref[...]) * scale2).T.astype(BF16)
    ka_ref[...] = norm_rope(ya[:, 512:640], kg_ref[...]).astype(BF16)
    vt = ya[:, 640:768].T.astype(BF16)
    for h in range(ATT_KV_HEADS):
        vt_ref[0, h, 0:HEAD_DIM, :] = vt[HEAD_DIM * h:HEAD_DIM * (h + 1)]
        vt_ref[0, h, HEAD_DIM:, :] = jnp.ones((GQA_ONES_ROWS, tm), BF16)

    yn = jnp.dot(xm, wn_ref[...], preferred_element_type=F32)
    for j in range(4, 8):
        conv_group(j)
    nq_ref[...] = (yn[:, 0:512] * scale2).astype(BF16)
    nkt_ref[0] = yn[:, 512:1024].T.astype(BF16)
    nv_ref[...] = yn[:, 1024:1536].astype(BF16)

    cz = jnp.dot(xm, wz_ref[...], preferred_element_type=F32)
    raw = jnp.dot(xm, wab_ref[...], preferred_element_type=F32)
    for j in range(8, 12):
        conv_group(j)
    cz_ref[...] = cz
    a_log = gp_ref[0:1, :]
    dt_b = gp_ref[1:2, :]
    is_g = gp_ref[2:3, :] > 0.5
    z = raw + dt_b
    softplus = jnp.maximum(z, 0.0) + jnp.log(1.0 + jnp.exp(-jnp.abs(z)))
    gn_ref[...] = jnp.where(is_g, -jnp.exp(a_log) * softplus, jax.nn.sigmoid(raw))


def _proj_call(x, bsz, seq, cos_t, sin_t, wa, wn, wc, wz, wab, qg, kg, cw, gp, tm=512):
    t = x.shape[0]
    nts = seq // tm
    nh = t // HALO
    row = lambda i: (i, 0)
    seqrow = lambda i: (i % nts, 0)
    in_specs = [
        pl.BlockSpec((HALO, D_MODEL), lambda i: (jnp.maximum(i * (tm // HALO) - 1, 0), 0)),
        pl.BlockSpec((tm, D_MODEL), row),
        pl.BlockSpec((HALO, D_MODEL), lambda i: (jnp.minimum((i + 1) * (tm // HALO), nh - 1), 0)),
        pl.BlockSpec((tm, LANES), seqrow),
        pl.BlockSpec((tm, LANES), seqrow),
        _const_spec(wa.shape), _const_spec(wn.shape), _const_spec(wc.shape), _const_spec(wz.shape),
        _const_spec(wab.shape), _const_spec(qg.shape), _const_spec(kg.shape), _const_spec(cw.shape),
        _const_spec(gp.shape),
    ]
    tposed = lambda i: (i // nts, 0, i % nts)
    out_specs = [
        pl.BlockSpec((1, 512, tm), tposed),
        pl.BlockSpec((tm, LANES), row),
        pl.BlockSpec((1, ATT_KV_HEADS, HEAD_DIM + GQA_ONES_ROWS, tm), lambda i: (i // nts, 0, 0, i % nts)),
        pl.BlockSpec((tm, 512), row),
        pl.BlockSpec((1, 512, tm), tposed),
        pl.BlockSpec((tm, 512), row),
        pl.BlockSpec((tm, 512), row),
        pl.BlockSpec((tm, 512), row),
        pl.BlockSpec((tm, 512), row),
        pl.BlockSpec((tm, 512), row),
        pl.BlockSpec((tm, LANES), row),
    ]
    out_shape = [
        jax.ShapeDtypeStruct((bsz, 512, seq), BF16),
        jax.ShapeDtypeStruct((t, LANES), BF16),
        jax.ShapeDtypeStruct((bsz, ATT_KV_HEADS, HEAD_DIM + GQA_ONES_ROWS, seq), BF16),
        jax.ShapeDtypeStruct((t, 512), BF16),
        jax.ShapeDtypeStruct((bsz, 512, seq), BF16),
        jax.ShapeDtypeStruct((t, 512), BF16),
        jax.ShapeDtypeStruct((t, 512), F32),
        jax.ShapeDtypeStruct((t, 512), F32),
        jax.ShapeDtypeStruct((t, 512), F32),
        jax.ShapeDtypeStruct((t, 512), F32),
        jax.ShapeDtypeStruct((t, LANES), F32),
    ]
    return pl.pallas_call(
        functools.partial(_proj_kernel, tm=tm, nts=nts),
        grid=(t // tm,),
        in_specs=in_specs,
        out_specs=out_specs,
        out_shape=out_shape,
        scratch_shapes=[pltpu.VMEM((tm + 2 * HALO, D_MODEL), BF16), pltpu.VMEM((tm + 2 * HALO, 1536), F32)],
        compiler_params=_cparams(("parallel",)),
        name="proj",
    )(x, x, x, cos_t, sin_t, wa, wn, wc, wz, wab, qg, kg, cw, gp)


GQA_TQ = 512
GQA_QSUB = 128
GQA_ONES_ROWS = 16
GQA_KEY_CHUNK = 1024


def _gqa_kernel(qt_ref, k_ref, vt_ref, o_ref):
    seq = k_ref.shape[1]
    kc = GQA_KEY_CHUNK
    nkc = seq // kc
    qs = GQA_QSUB
    zeros = jnp.zeros((HEAD_DIM, 2 * qs), BF16)
    groups = [(h, 4 * h + 2 * pr, t) for t in range(GQA_TQ // qs) for h in range(ATT_KV_HEADS) for pr in range(2)]

    def scores(h, g0, t):
        cols = slice(qs * t, qs * (t + 1))
        w = jnp.concatenate([qt_ref[0, HEAD_DIM * g0:HEAD_DIM * (g0 + 1), cols],
                             qt_ref[0, HEAD_DIM * (g0 + 1):HEAD_DIM * (g0 + 2), cols]], axis=1)
        w = jnp.concatenate([w, zeros] if h == 0 else [zeros, w], axis=0)
        sts = [jnp.dot(k_ref[0, kc * c:kc * (c + 1), :], w, preferred_element_type=F32) for c in range(nkc)]
        m = functools.reduce(jnp.maximum, [jnp.max(st, axis=0, keepdims=True) for st in sts])
        return sts, m

    def outputs(h, g0, t, sts, m):
        ot = None
        for c in range(nkc):
            pt = jnp.exp2(sts[c] - m).astype(BF16)
            part = jnp.dot(vt_ref[0, h, :, kc * c:kc * (c + 1)], pt, preferred_element_type=F32)
            ot = part if ot is None else ot + part
        on = ot[:HEAD_DIM] * (1.0 / ot[HEAD_DIM:HEAD_DIM + 1])
        both = jnp.concatenate([on[:, :qs], on[:, qs:]], axis=0)
        o_ref[0, qs * t:qs * (t + 1), HEAD_DIM * g0:HEAD_DIM * (g0 + 2)] = both.T.astype(BF16)

    pending = scores(*groups[0])
    for n, grp in enumerate(groups):
        nxt = scores(*groups[n + 1]) if n + 1 < len(groups) else None
        outputs(*grp, *pending)
        pending = nxt


def _gqa_call(qt, ka, vt):
    bsz, _, seq = qt.shape
    return pl.pallas_call(
        _gqa_kernel,
        grid=(bsz, seq // GQA_TQ),
        in_specs=[
            pl.BlockSpec((1, 512, GQA_TQ), lambda b, i: (b, 0, i)),
            pl.BlockSpec((1, seq, LANES), lambda b, i: (b, 0, 0)),
            pl.BlockSpec((1,) + vt.shape[1:], lambda b, i: (b, 0, 0, 0)),
        ],
        out_specs=pl.BlockSpec((1, GQA_TQ, 512), lambda b, i: (b, i, 0)),
        out_shape=jax.ShapeDtypeStruct((bsz, seq, 512), BF16),
        compiler_params=_cparams(("parallel", "arbitrary")),
        name="gqa",
    )(qt, ka, vt)


NA_QBLK = 128


def _na_geometry(seq):
    rows = seq // GRID_W
    wr = min(NA_WIN_ROWS, rows)
    wc = NA_WIN_COLS
    qrows = NA_QBLK // GRID_W
    nblk = seq // NA_QBLK
    nkb = min(5, nblk)
    krows = nkb * qrows
    starts = np.clip(np.arange(nblk) - 2, 0, nblk - nkb)
    variants, var_of = [], []
    for i in range(nblk):
        r = i * qrows + np.arange(qrows)
        r0 = np.clip(r - wr // 2, 0, rows - wr)
        kr = starts[i] * qrows + np.arange(krows)
        valid = (kr[None] >= r0[:, None]) & (kr[None] < r0[:, None] + wr)
        assert (valid.sum(1) == wr).all()
        dr = kr[None] - r[:, None] + (NA_WIN_ROWS - 1)
        sel = (valid[..., None] & (dr[..., None] == np.arange(2 * NA_WIN_ROWS - 1))).astype(np.float32)
        key = sel.tobytes()
        if key not in variants:
            variants.append(key)
        var_of.append(variants.index(key))
    row_sel = np.stack([np.frombuffer(k, np.float32).reshape(qrows, krows, 2 * NA_WIN_ROWS - 1) for k in variants])
    c = np.arange(GRID_W)
    c0 = np.clip(c - wc // 2, 0, GRID_W - wc)
    validc = (c[None] >= c0[:, None]) & (c[None] < c0[:, None] + wc)
    dc = c[None] - c[:, None] + (NA_WIN_COLS - 1)
    col_sel = (validc[..., None] & (dc[..., None] == np.arange(2 * NA_WIN_COLS - 1))).astype(np.float32)
    return nkb, starts.astype(np.int32), np.asarray(var_of, np.int32), row_sel, col_sel


def _na_bias_table(rpb, row_sel, col_sel):
    hi = lax.Precision.HIGHEST
    cols = jnp.einsum('hrd,cnd->hrcn', rpb, col_sel, precision=hi)
    tab = jnp.einsum('vqkr,hrcn->vhqckn', row_sel, cols, precision=hi)
    valid = (row_sel.sum(-1) > 0)[:, None, :, None, :, None] & (col_sel.sum(-1) > 0)[None, None, None, :, None, :]
    tab = jnp.where(valid, tab * LOG2E, NEG)
    nv, h = tab.shape[:2]
    return tab.reshape(nv, h // 2, 2 * NA_QBLK, -1)


def _na_kernel(var_ref, kb_ref, q_ref, *refs, nkb):
    kt_refs = refs[:nkb]
    v_refs = refs[nkb:2 * nkb]
    bias_ref = refs[2 * nkb]
    o_ref = refs[2 * nkb + 1]
    lane = lax.broadcasted_iota(jnp.int32, (NA_QBLK, LANES), 1)
    lo = lane < HEAD_DIM
    def scores(pp):
        sl = slice(LANES * pp, LANES * (pp + 1))
        qp = q_ref[0, :, sl]
        zero = jnp.zeros_like(qp)
        q2 = jnp.concatenate([jnp.where(lo, qp, zero), jnp.where(lo, zero, qp)], axis=0)
        ktp = jnp.concatenate([r[0, sl, :] for r in kt_refs], axis=1)
        return jnp.dot(q2, ktp, preferred_element_type=F32) + bias_ref[0, pp]

    def outputs(pp, s):
        sl = slice(LANES * pp, LANES * (pp + 1))
        vp = jnp.concatenate([r[0, :, sl] for r in v_refs], axis=0)
        m = jnp.max(s, axis=-1, keepdims=True)
        p = jnp.exp2(s - m)
        l = jnp.sum(p, axis=-1, keepdims=True)
        pv = jnp.dot(p.astype(BF16), vp, preferred_element_type=F32) / l
        o_ref[0, :, sl] = jnp.where(lo, pv[:NA_QBLK], pv[NA_QBLK:]).astype(BF16)

    npairs = NA_HEADS // 2
    pending = scores(0)
    for pp in range(npairs):
        nxt = scores(pp + 1) if pp + 1 < npairs else None
        outputs(pp, pending)
        pending = nxt


def _na_call(nq, nkt, nv, rpb):
    bsz, seq, _ = nq.shape
    nkb, starts, var_of, row_sel, col_sel = _na_geometry(seq)
    bias = _na_bias_table(rpb.astype(F32), row_sel, col_sel)
    nblk = seq // NA_QBLK
    in_specs = [pl.BlockSpec((1, NA_QBLK, 512), lambda b, i, var, kb: (b, i, 0))]
    for j in range(nkb):
        in_specs.append(pl.BlockSpec((1, 512, NA_QBLK), lambda b, i, var, kb, j=j: (b, 0, kb[i] + j)))
    for j in range(nkb):
        in_specs.append(pl.BlockSpec((1, NA_QBLK, 512), lambda b, i, var, kb, j=j: (b, kb[i] + j, 0)))
    in_specs.append(pl.BlockSpec((1,) + bias.shape[1:], lambda b, i, var, kb: (var[i], 0, 0, 0)))
    grid_spec = pltpu.PrefetchScalarGridSpec(
        num_scalar_prefetch=2,
        grid=(bsz, nblk),
        in_specs=in_specs,
        out_specs=pl.BlockSpec((1, NA_QBLK, 512), lambda b, i, var, kb: (b, i, 0)),
    )
    return pl.pallas_call(
        functools.partial(_na_kernel, nkb=nkb),
        grid_spec=grid_spec,
        out_shape=jax.ShapeDtypeStruct((bsz, seq, 512), BF16),
        compiler_params=_cparams(("parallel", "arbitrary")),
        name="na",
    )(jnp.asarray(var_of), jnp.asarray(starts), nq, *([nkt] * nkb), *([nv] * nkb), bias)


DN_TC = 512
DN_NHP = 4


def _bd(x, lo):
    xb = x.astype(BF16)
    zero = jnp.zeros_like(xb)
    return jnp.concatenate([jnp.where(lo, xb, zero), jnp.where(lo, zero, xb)], axis=-2)


def _bmm(a, b16):
    return jnp.einsum('bik,bkj->bij', a.astype(BF16), b16, preferred_element_type=F32)


def _dn_prepare(d, hp, q, k, v, gn, gr):
    C = DN_CHUNK
    nch = DN_TC // C
    sgn = 1 - 2 * d

    col = lax.broadcasted_iota(jnp.int32, (LANES, 2 * LANES), 0)
    ln2 = lax.broadcasted_iota(jnp.int32, (LANES, 2 * LANES), 1)
    want = (ln2 // LANES) * 16 + d * 8 + 2 * hp + (ln2 % LANES) // HEAD_DIM
    onehot = jnp.where(col == want, 1.0, 0.0).astype(BF16)
    gb = _dot_exact_rhs(gn, onehot)
    g_col = gb[:, :LANES]
    beta = gb[:, LANES:]
    pos = lax.broadcasted_iota(jnp.int32, (DN_TC, LANES), 0) % C
    gc = g_col
    shift = 1
    while shift < C:
        if d == 0:
            gc = gc + jnp.where(pos >= shift, pltpu.roll(gc, shift, 0), 0.0)
        else:
            gc = gc + jnp.where(pos < C - shift, pltpu.roll(gc, DN_TC - shift, 0), 0.0)
        shift *= 2
    last = C - 1 if d == 0 else 0
    tot = jnp.concatenate([jnp.broadcast_to(gc[C * j + last:C * j + last + 1, :], (C, LANES)) for j in range(nch)], axis=0)
    r2 = lax.broadcasted_iota(jnp.int32, (LANES, LANES), 0)
    c2 = lax.broadcasted_iota(jnp.int32, (LANES, LANES), 1)
    tri2 = jnp.where((r2 // C == c2 // C) & ((c2 - r2) * sgn >= 0), 1.0, 0.0).astype(BF16)
    gcr = _dot_exact_rhs(gr, tri2)

    egc = jnp.exp(gc)
    kbeta = k * beta
    vbeta = v * beta
    kbd = kbeta * egc
    qd = q * egc
    kd = k * jnp.exp(tot - gc)
    etot = jnp.exp(tot)

    row = lax.broadcasted_iota(jnp.int32, (C, LANES), 0)
    jj = lax.broadcasted_iota(jnp.int32, (C, LANES), 1) % HEAD_DIM
    earlier_eq = (row - jj) * sgn >= 0
    strict = jnp.where((row - jj) * sgn > 0, 1.0, 0.0)

    def chunks(a):
        return a.reshape(nch, C, a.shape[-1])

    gc3 = chunks(gc)
    diff = jnp.stack([gc3[j] - gcr[j:j + 1, :] for j in range(nch)])
    decay = jnp.where(earlier_eq, jnp.exp(jnp.minimum(diff, 0.0)), 0.0)
    return dict(q=chunks(q), k=chunks(k), kbeta=chunks(kbeta), vbeta=chunks(vbeta), kbd=chunks(kbd), qd=chunks(qd),
                kd=chunks(kd), etot=chunks(etot), decay=decay, decay_strict=decay * strict)


def _dn_solve(e):
    C = DN_CHUNK
    row = lax.broadcasted_iota(jnp.int32, (C, LANES), 0)
    lane = lax.broadcasted_iota(jnp.int32, (C, LANES), 1)
    lo = lane < HEAD_DIM
    eye2 = jnp.where(row == lane % HEAD_DIM, 1.0, 0.0)
    ai = jnp.einsum('bik,bjk->bij', jnp.concatenate([e['kbeta'], e['q']], axis=1).astype(BF16), _bd(e['k'], lo),
                    preferred_element_type=F32)
    intra = ai[:, C:] * e['decay']
    x = -ai[:, :C] * e['decay_strict']
    t = eye2 + x
    p = _bmm(x, _bd(x, lo))
    for lvl in range(5):
        pbd = _bd(p, lo)
        if lvl < 4:
            rr = _bmm(jnp.concatenate([p, t], axis=1), pbd)
            p = rr[:, :C]
            t = t + rr[:, C:]
        else:
            t = t + _bmm(t, pbd)
    uw = _bmm(t, jnp.concatenate([_bd(e['vbeta'], lo), _bd(e['kbd'], lo)], axis=2))
    kdt = jnp.stack([e['kd'][b].T for b in range(e['kd'].shape[0])]).astype(BF16)
    r3 = lax.broadcasted_iota(jnp.int32, (LANES, 2 * LANES), 0) // HEAD_DIM
    c3 = (lax.broadcasted_iota(jnp.int32, (LANES, 2 * LANES), 1) % LANES) // HEAD_DIM
    nm = jnp.where(r3 == c3, _bmm(kdt, uw.astype(BF16)), 0.0)
    return uw, intra, nm


def _dn_kernel(qf_ref, kf_ref, vf_ref, gnf_ref, grf_ref, qb_ref, kb_ref, vb_ref, gnb_ref, grb_ref,
               of_ref, ob_ref, s_ref):
    C = DN_CHUNK
    nch = DN_TC // C

    @pl.when(pl.program_id(2) == 0)
    def _():
        s_ref[...] = jnp.zeros_like(s_ref)

    in_refs = ((qf_ref, kf_ref, vf_ref, gnf_ref, grf_ref), (qb_ref, kb_ref, vb_ref, gnb_ref, grb_ref))
    out_refs = (of_ref, ob_ref)
    lo = lax.broadcasted_iota(jnp.int32, (C, LANES), 1) < HEAD_DIM

    chains = [(d, i) for d in range(2) for i in range(DN_NHP)]
    per = []
    for d, i in chains:
        q_ref, k_ref, v_ref, gn_ref, gr_ref = in_refs[d]
        sl = slice(LANES * i, LANES * (i + 1))
        hp = pl.program_id(1) * DN_NHP + i
        per.append(_dn_prepare(d, hp, q_ref[0, :, sl], k_ref[0, :, sl], v_ref[0, :, sl], gn_ref[0], gr_ref[0, i, 0]))
    e = {name: jnp.concatenate([p[name] for p in per], axis=0) for name in per[0]}
    uw, intra, nm = _dn_solve(e)
    qd, etot = e['qd'], e['etot']
    states = [s_ref[d, i] for d, i in chains]
    for step in range(nch):
        for n, (d, i) in enumerate(chains):
            j = step if d == 0 else nch - 1 - step
            b = n * nch + j
            s16 = states[n].astype(BF16)
            states[n] = (states[n] * etot[b, 0:1] + nm[b, :, :LANES]
                         - jnp.dot(nm[b, :, LANES:].astype(BF16), s16, preferred_element_type=F32))
            wq = jnp.dot(jnp.concatenate([uw[b, :, LANES:], qd[b]], axis=0).astype(BF16), s16,
                         preferred_element_type=F32)
            v_new = uw[b, :, :LANES] - wq[:C]
            out_refs[d][0, C * j:C * (j + 1), LANES * i:LANES * (i + 1)] = wq[C:] + jnp.dot(
                intra[b].astype(BF16), _bd(v_new, lo), preferred_element_type=F32)
    for n, (d, i) in enumerate(chains):
        s_ref[d, i] = states[n]


def _dn_call(dq, dk, dv, gn, gr):
    bsz, seq, _ = dq.shape
    nsc = seq // DN_TC
    nch = DN_TC // DN_CHUNK
    width = LANES * DN_NHP
    in_specs = []
    for d in range(2):
        tmap = (lambda c: c) if d == 0 else (lambda c: nsc - 1 - c)
        qkv_spec = pl.BlockSpec((1, DN_TC, width), lambda b, hp, c, tmap=tmap: (b, tmap(c), hp))
        in_specs += [
            qkv_spec, qkv_spec, qkv_spec,
            pl.BlockSpec((1, DN_TC, LANES), lambda b, hp, c, tmap=tmap: (b, tmap(c), 0)),
            pl.BlockSpec((1, DN_NHP, 1, nch, LANES), lambda b, hp, c, tmap=tmap, d=d: (d, hp, b, tmap(c), 0)),
        ]
    out_specs = [
        pl.BlockSpec((1, DN_TC, width), lambda b, hp, c: (b, c, hp)),
        pl.BlockSpec((1, DN_TC, width), lambda b, hp, c: (b, nsc - 1 - c, hp)),
    ]
    return pl.pallas_call(
        _dn_kernel,
        grid=(bsz, DN_HEADS // 2 // DN_NHP, nsc),
        in_specs=in_specs,
        out_specs=out_specs,
        out_shape=[jax.ShapeDtypeStruct((bsz, seq, 512), F32)] * 2,
        scratch_shapes=[pltpu.VMEM((2, DN_NHP, LANES, LANES), F32)],
        compiler_params=_cparams(("parallel", "parallel", "arbitrary")),
        name="deltanet",
    )(dq, dk, dv, gn, gr, dq, dk, dv, gn, gr)


def _dn_rows(gn, bsz, seq):
    g = gn[:, :32].reshape(bsz, seq // DN_CHUNK, DN_CHUNK, 2, 2, DN_HEADS // 2, 2)
    g = g[:, :, :, 0]
    g = jnp.transpose(g, (3, 4, 0, 1, 5, 2))
    return g.reshape(2, DN_HEADS // 2, bsz, seq // DN_CHUNK, LANES)


def _merge_kernel(x_ref, bra_ref, brb_ref, of_ref, ob_ref, cz_ref, wg_ref, bg_ref, wb_ref, wo_ref,
                  ng_ref, g1_ref, b1_ref, o_ref):
    x = x_ref[...]
    xb = x.astype(BF16)
    ones = _seg_ones(LANES, HEAD_DIM)
    parts = []
    for j in range(4):
        sl = slice(LANES * j, LANES * (j + 1))
        o = of_ref[:, sl] + ob_ref[:, sl]
        ss = _seg_sum(o * o, ones)
        parts.append((o * lax.rsqrt(ss * (1.0 / HEAD_DIM) + RMS_EPS) * ng_ref[...] * _silu(cz_ref[:, sl])).astype(BF16))
    brc = jnp.concatenate(parts, axis=1)
    merged = None
    for n, br in enumerate((bra_ref[...], brb_ref[...], brc)):
        sl = slice(D_MODEL * n, D_MODEL * (n + 1))
        gate = jax.nn.sigmoid(jnp.dot(xb, wg_ref[:, sl], preferred_element_type=F32) + bg_ref[:, sl])
        term = gate * jnp.dot(br, wb_ref[n], preferred_element_type=F32)
        merged = term if merged is None else merged + term
    y = DEEPNORM_ALPHA * x + jnp.dot(merged.astype(BF16), wo_ref[...], preferred_element_type=F32)
    o_ref[...] = _layer_norm(y, g1_ref[...], b1_ref[...])


def _merge_call(x, bra, brb, o_f, o_b, cz, wg, bg, wb, wo, ng, g1, b1, tm=512):
    t = x.shape[0]
    row = lambda i: (i, 0)
    return pl.pallas_call(
        _merge_kernel,
        grid=(t // tm,),
        in_specs=[
            pl.BlockSpec((tm, D_MODEL), row),
            pl.BlockSpec((tm, 512), row),
            pl.BlockSpec((tm, 512), row),
            pl.BlockSpec((tm, 512), row),
            pl.BlockSpec((tm, 512), row),
            pl.BlockSpec((tm, 512), row),
            _const_spec(wg.shape), _const_spec(bg.shape), _const_spec(wb.shape), _const_spec(wo.shape),
            _const_spec(ng.shape), _const_spec(g1.shape), _const_spec(b1.shape),
        ],
        out_specs=pl.BlockSpec((tm, D_MODEL), row),
        out_shape=jax.ShapeDtypeStruct((t, D_MODEL), F32),
        compiler_params=_cparams(("parallel",)),
        name="merge",
    )(x, bra, brb, o_f, o_b, cz, wg, bg, wb, wo, ng, g1, b1)


def _ffn_kernel(x_ref, pa_ref, pb_ref, wfg_ref, wfu_ref, wfo_ref, wpg_ref, bpg_ref, wpp_ref, g2_ref, b2_ref, o_ref,
                *, n_first):
    x = x_ref[...]
    xb = x.astype(BF16)
    p = jnp.where(pl.program_id(0) < n_first, pa_ref[0], pb_ref[0])
    gate = jnp.dot(xb, wfg_ref[...], preferred_element_type=F32)
    up = jnp.dot(xb, wfu_ref[...], preferred_element_type=F32)
    ffn = jnp.dot((_silu(gate) * up).astype(BF16), wfo_ref[...], preferred_element_type=F32)
    ple = (jax.nn.sigmoid(jnp.dot(xb, wpg_ref[...], preferred_element_type=F32) + bpg_ref[...])
           * jnp.dot(p.astype(BF16), wpp_ref[...], preferred_element_type=F32))
    o_ref[...] = _layer_norm(DEEPNORM_ALPHA * x + ffn + ple, g2_ref[...], b2_ref[...])


def _ffn_call(x, pa, pb, layer, wfg, wfu, wfo, wpg, bpg, wpp, g2, b2, tm=256):
    t = x.shape[0]
    n_first = pa.shape[1] // tm
    row = lambda i: (i, 0)
    return pl.pallas_call(
        functools.partial(_ffn_kernel, n_first=n_first),
        grid=(t // tm,),
        in_specs=[
            pl.BlockSpec((tm, D_MODEL), row),
            pl.BlockSpec((1, tm, PLE_DIM), lambda i: (layer, jnp.minimum(i, n_first - 1), 0)),
            pl.BlockSpec((1, tm, PLE_DIM), lambda i: (layer, jnp.maximum(i - n_first, 0), 0)),
            _const_spec(wfg.shape), _const_spec(wfu.shape), _const_spec(wfo.shape), _const_spec(wpg.shape),
            _const_spec(bpg.shape), _const_spec(wpp.shape), _const_spec(g2.shape), _const_spec(b2.shape),
        ],
        out_specs=pl.BlockSpec((tm, D_MODEL), row),
        out_shape=jax.ShapeDtypeStruct((t, D_MODEL), F32),
        compiler_params=_cparams(("arbitrary",)),
        name="ffn",
    )(x, pa, pb, wfg, wfu, wfo, wpg, bpg, wpp, g2, b2)


def _rope_tables(seq):
    t = jnp.arange(seq, dtype=jnp.int32)
    rowf = (t // GRID_W).astype(F32)
    colf = (t % GRID_W).astype(F32)
    inv_freq = ROPE_THETA ** (-jnp.arange(ROPE_PAIRS, dtype=F32) / ROPE_PAIRS)
    ang_r = rowf[:, None] * inv_freq[None, :]
    ang_c = colf[:, None] * inv_freq[None, :]
    cos = jnp.concatenate([jnp.cos(ang_r), jnp.cos(ang_r), jnp.cos(ang_c), jnp.cos(ang_c)], axis=1)
    sin = jnp.concatenate([-jnp.sin(ang_r), jnp.sin(ang_r), -jnp.sin(ang_c), jnp.sin(ang_c)], axis=1)
    return jnp.tile(cos, (1, 2)), jnp.tile(sin, (1, 2))


def _tile_lanes(v, reps):
    return jnp.tile(v.astype(F32), reps).reshape(1, -1)


def _layer(x, pa, pb, layer, bsz, seq, tables, w):
    (w_in, qn_g, kn_g, rpb, conv_w, a_log, dt_bias, dn_g, w_gate, b_gate, w_branch, w_out, ln1_g, ln1_b,
     w_ffn_in, w_ffn_out, w_ple_gate, b_ple_gate, w_ple_proj, ln2_g, ln2_b) = w
    o = _OFF
    wa = w_in[:, o[0]:o[3]].astype(BF16)
    wn = w_in[:, o[3]:o[6]].astype(BF16)
    wc = w_in[:, o[6]:o[9]].astype(BF16)
    wz = w_in[:, o[9]:o[10]].astype(BF16)
    wab = jnp.pad(w_in[:, o[10]:o[14]], ((0, 0), (0, LANES - 32))).astype(BF16)
    gp = jnp.zeros((8, LANES), F32)
    gp = gp.at[0, :16].set(a_log.reshape(16).astype(F32))
    gp = gp.at[1, :16].set(dt_bias.reshape(16).astype(F32))
    gp = gp.at[2, :16].set(1.0)
    cos_t, sin_t = tables
    (qt, ka, vt, nq, nkt, nv, dq, dk, dv, cz, gn) = _proj_call(
        x, bsz, seq, cos_t, sin_t, wa, wn, wc, wz, wab,
        _tile_lanes(qn_g, 2), _tile_lanes(kn_g, 2), conv_w.astype(F32), gp)
    r3 = lambda a: a.reshape(bsz, seq, a.shape[-1])
    bra = _gqa_call(qt, r3(ka), vt)
    brb = _na_call(r3(nq), nkt, r3(nv), rpb)
    o_f, o_b = _dn_call(r3(dq), r3(dk), r3(dv), r3(gn), _dn_rows(gn, bsz, seq))
    t = bsz * seq
    x = _merge_call(x, bra.reshape(t, 512), brb.reshape(t, 512), o_f.reshape(t, 512), o_b.reshape(t, 512), cz,
                    w_gate.astype(BF16), b_gate.astype(F32).reshape(1, -1), w_branch.astype(BF16),
                    w_out.astype(BF16), _tile_lanes(dn_g, 2), ln1_g.astype(F32).reshape(1, -1),
                    ln1_b.astype(F32).reshape(1, -1))
    x = _ffn_call(x, pa, pb, layer, w_ffn_in[:, :FFN_HIDDEN].astype(BF16), w_ffn_in[:, FFN_HIDDEN:].astype(BF16),
                  w_ffn_out.astype(BF16), w_ple_gate.astype(BF16), b_ple_gate.astype(F32).reshape(1, -1),
                  w_ple_proj.astype(BF16), ln2_g.astype(F32).reshape(1, -1), ln2_b.astype(F32).reshape(1, -1))
    return x


def _trunk(xa, xb, pa, pb, emb_ln_g, emb_ln_b, weights):
    (na, seq, d), nb = xa.shape, xb.shape[0]
    depth = pa.shape[0]
    bsz = na + nb
    xf = _ln_call(xa.reshape(na * seq, d), xb.reshape(nb * seq, d), emb_ln_g.astype(F32), emb_ln_b.astype(F32))
    pa = pa.reshape(depth, na * seq, -1)
    pb = pb.reshape(depth, nb * seq, -1)
    tables = _rope_tables(seq)
    for i in range(depth):
        xf = _layer(xf, pa, pb, i, bsz, seq, tables, tuple(a[i] for a in weights))
    y = xf.reshape(bsz, seq, d)
    return y[:na], y[na:]


def kernel(x_prompt, x_sample, p_prompt, p_sample, emb_ln_g, emb_ln_b, w_in, att_q_norm_g, att_k_norm_g, na_rpb, dn_conv_w, dn_a_log, dn_dt_bias, dn_norm_g, w_gate, b_gate, w_branch, w_out, ln1_g, ln1_b, w_ffn_in, w_ffn_out, w_ple_gate, b_ple_gate, w_ple_proj, ln2_g, ln2_b):
    weights = (w_in, att_q_norm_g, att_k_norm_g, na_rpb, dn_conv_w, dn_a_log, dn_dt_bias, dn_norm_g, w_gate,
               b_gate, w_branch, w_out, ln1_g, ln1_b, w_ffn_in, w_ffn_out, w_ple_gate, b_ple_gate, w_ple_proj,
               ln2_g, ln2_b)
    return _trunk(x_prompt, x_sample, p_prompt, p_sample, emb_ln_g, emb_ln_b, weights)
```

```python
import functools

import numpy as np
import jax
import jax.numpy as jnp
from jax import lax
from jax.experimental import pallas as pl
from jax.experimental.pallas import tpu as pltpu

F32 = jnp.float32
BF16 = jnp.bfloat16

D_MODEL = 1024
DEPTH = 4
GRID_W = 64
HEAD_DIM = 64
ATT_Q_HEADS = 8
ATT_KV_HEADS = 2
ROPE_PAIRS = HEAD_DIM // 4
ROPE_THETA = 10000.0
NA_HEADS = 8
NA_WIN_ROWS = 8
NA_WIN_COLS = 16
DN_HEADS = 8
DN_CHUNK = 64
BRANCH_WIDTH = 512
FFN_HIDDEN = 2816
PLE_DIM = 256
LN_EPS = 1e-5
RMS_EPS = 1e-6
L2_EPS = 1e-6
DEEPNORM_ALPHA = (2 * DEPTH) ** 0.25

LANES = 128
HALO = 16
VMEM_LIMIT = 56 * 1024 * 1024
NEG = -1e30
LOG2E = 1.4426950408889634

_OFF = np.cumsum([0, 512, 128, 128, 512, 512, 512, 512, 512, 512, 512, 8, 8, 8, 8])


def _cparams(sem):
    return pltpu.CompilerParams(dimension_semantics=sem, vmem_limit_bytes=VMEM_LIMIT)


def _const_spec(shape):
    nd = len(shape)
    return pl.BlockSpec(shape, lambda *a: (0,) * nd, pipeline_mode=pl.Buffered(1))


def _bdot(a, b):
    return jnp.dot(a.astype(BF16), b.astype(BF16), preferred_element_type=F32)


def _split(x, n):
    out = []
    r = x
    for _ in range(n):
        p = r.astype(BF16)
        out.append(p)
        r = r - p.astype(F32)
    return out


def _dot_exact_rhs(a, b01, n=3):
    acc = None
    for p in _split(a, n):
        t = jnp.dot(p, b01, preferred_element_type=F32)
        acc = t if acc is None else acc + t
    return acc


def _dot_exact_lhs(a01, b, n=3):
    acc = None
    for p in _split(b, n):
        t = jnp.dot(a01, p, preferred_element_type=F32)
        acc = t if acc is None else acc + t
    return acc


def _seg_ones(n, seg):
    r = lax.broadcasted_iota(jnp.int32, (n, n), 0) // seg
    c = lax.broadcasted_iota(jnp.int32, (n, n), 1) // seg
    return jnp.where(r == c, 1.0, 0.0).astype(BF16)


def _seg_sum(x, ones):
    return _dot_exact_rhs(x, ones, 1)


def _layer_norm(y, g, b):
    mu = jnp.mean(y, axis=-1, keepdims=True)
    yc = y - mu
    var = jnp.mean(yc * yc, axis=-1, keepdims=True)
    return yc * lax.rsqrt(var + LN_EPS) * g + b


def _silu(x):
    return x * jax.nn.sigmoid(x)


def _two_group_specs(block, n_first):
    nd = len(block)
    tail = (0,) * (nd - 1)
    return [pl.BlockSpec(block, lambda i: (jnp.minimum(i, n_first - 1),) + tail),
            pl.BlockSpec(block, lambda i: (jnp.maximum(i - n_first, 0),) + tail)]


def _ln_kernel(xa_ref, xb_ref, g_ref, b_ref, o_ref, *, n_first):
    x = jnp.where(pl.program_id(0) < n_first, xa_ref[...], xb_ref[...])
    o_ref[...] = _layer_norm(x, g_ref[...], b_ref[...])


def _ln_call(xa, xb, g, b, tm=1024):
    d = xa.shape[1]
    t = xa.shape[0] + xb.shape[0]
    n_first = xa.shape[0] // tm
    return pl.pallas_call(
        functools.partial(_ln_kernel, n_first=n_first),
        grid=(t // tm,),
        in_specs=_two_group_specs((tm, d), n_first) + [_const_spec((1, d)), _const_spec((1, d))],
        out_specs=pl.BlockSpec((tm, d), lambda i: (i, 0)),
        out_shape=jax.ShapeDtypeStruct((t, d), F32),
        compiler_params=_cparams(("arbitrary",)),
        name="emb_ln",
    )(xa, xb, g.reshape(1, d), b.reshape(1, d))


def _proj_kernel(xp_ref, x_ref, xn_ref, cos_ref, sin_ref, wa_ref, wn_ref, wc_ref, wz_ref, wab_ref,
                 qg_ref, kg_ref, cw_ref, gp_ref,
                 qt_ref, ka_ref, vt_ref, nq_ref, nkt_ref, nv_ref, dq_ref, dk_ref, dv_ref, cz_ref, gn_ref,
                 xs_ref, c_ref, *, tm, nts):
    i = pl.program_id(0)
    first = (i % nts) == 0
    last = (i % nts) == nts - 1
    xs_ref[0:HALO, :] = jnp.where(first, 0.0, xp_ref[...]).astype(BF16)
    xs_ref[HALO:HALO + tm, :] = x_ref[...].astype(BF16)
    xs_ref[HALO + tm:, :] = jnp.where(last, 0.0, xn_ref[...]).astype(BF16)
    xm = xs_ref[HALO:HALO + tm, :]

    ones = _seg_ones(LANES, HEAD_DIM)
    cos = cos_ref[...]
    sin = sin_ref[...]
    lane = lax.broadcasted_iota(jnp.int32, (tm, LANES), 1)
    first_half = (lane % (2 * ROPE_PAIRS)) < ROPE_PAIRS

    def norm_rope(xq, g):
        ss = _seg_sum(xq * xq, ones)
        xn = xq * lax.rsqrt(ss * (1.0 / HEAD_DIM) + RMS_EPS) * g
        partner = jnp.where(first_half, pltpu.roll(xn, LANES - ROPE_PAIRS, 1), pltpu.roll(xn, ROPE_PAIRS, 1))
        return xn * cos + partner * sin

    scale = HEAD_DIM ** -0.5
    scale2 = scale * LOG2E

    c_ref[...] = jnp.dot(xs_ref[...], wc_ref[...], preferred_element_type=F32)

    def conv_group(j):
        sl = slice(LANES * j, LANES * (j + 1))
        y = (c_ref[HALO - 1:HALO - 1 + tm, sl] * cw_ref[0:1, sl]
             + c_ref[HALO:HALO + tm, sl] * cw_ref[1:2, sl]
             + c_ref[HALO + 1:HALO + 1 + tm, sl] * cw_ref[2:3, sl])
        y = _silu(y)
        if j < 8:
            y = y * lax.rsqrt(_seg_sum(y * y, ones) + L2_EPS)
        if j < 4:
            y = y * scale
        osl = slice(LANES * (j % 4), LANES * (j % 4 + 1))
        (dq_ref, dk_ref, dv_ref)[j // 4][:, osl] = y

    ya = jnp.dot(xm, wa_ref[...], preferred_element_type=F32)
    for j in range(0, 4):
        conv_group(j)
    for j in range(4):
        sl = slice(LANES * j, LANES * (j + 1))
        qt_ref[0, sl, :] = (norm_rope(ya[:, sl], qg_ref[...]) * scale2).T.astype(BF16)
    ka_ref[...] = norm_rope(ya[:, 512:640], kg_ref[...]).astype(BF16)
    vt = ya[:, 640:768].T.astype(BF16)
    for h in range(ATT_KV_HEADS):
        vt_ref[0, h, 0:HEAD_DIM, :] = vt[HEAD_DIM * h:HEAD_DIM * (h + 1)]
        vt_ref[0, h, HEAD_DIM:, :] = jnp.ones((GQA_ONES_ROWS, tm), BF16)

    yn = jnp.dot(xm, wn_ref[...], preferred_element_type=F32)
    for j in range(4, 8):
        conv_group(j)
    nq_ref[...] = (yn[:, 0:512] * scale2).astype(BF16)
    nkt_ref[0] = yn[:, 512:1024].T.astype(BF16)
    nv_ref[...] = yn[:, 1024:1536].astype(BF16)

    cz = jnp.dot(xm, wz_ref[...], preferred_element_type=F32)
    raw = jnp.dot(xm, wab_ref[...], preferred_element_type=F32)
    for j in range(8, 12):
        conv_group(j)
    cz_ref[...] = cz
    a_log = gp_ref[0:1, :]
    dt_b = gp_ref[1:2, :]
    is_g = gp_ref[2:3, :] > 0.5
    z = raw + dt_b
    softplus = jnp.maximum(z, 0.0) + jnp.log(1.0 + jnp.exp(-jnp.abs(z)))
    gn_ref[...] = jnp.where(is_g, -jnp.exp(a_log) * softplus, jax.nn.sigmoid(raw))


def _proj_call(x, bsz, seq, cos_t, sin_t, wa, wn, wc, wz, wab, qg, kg, cw, gp, tm=512):
    t = x.shape[0]
    nts = seq // tm
    nh = t // HALO
    row = lambda i: (i, 0)
    seqrow = lambda i: (i % nts, 0)
    in_specs = [
        pl.BlockSpec((HALO, D_MODEL), lambda i: (jnp.maximum(i * (tm // HALO) - 1, 0), 0)),
        pl.BlockSpec((tm, D_MODEL), row),
        pl.BlockSpec((HALO, D_MODEL), lambda i: (jnp.minimum((i + 1) * (tm // HALO), nh - 1), 0)),
        pl.BlockSpec((tm, LANES), seqrow),
        pl.BlockSpec((tm, LANES), seqrow),
        _const_spec(wa.shape), _const_spec(wn.shape), _const_spec(wc.shape), _const_spec(wz.shape),
        _const_spec(wab.shape), _const_spec(qg.shape), _const_spec(kg.shape), _const_spec(cw.shape),
        _const_spec(gp.shape),
    ]
    tposed = lambda i: (i // nts, 0, i % nts)
    out_specs = [
        pl.BlockSpec((1, 512, tm), tposed),
        pl.BlockSpec((tm, LANES), row),
        pl.BlockSpec((1, ATT_KV_HEADS, HEAD_DIM + GQA_ONES_ROWS, tm), lambda i: (i // nts, 0, 0, i % nts)),
        pl.BlockSpec((tm, 512), row),
        pl.BlockSpec((1, 512, tm), tposed),
        pl.BlockSpec((tm, 512), row),
        pl.BlockSpec((tm, 512), row),
        pl.BlockSpec((tm, 512), row),
        pl.BlockSpec((tm, 512), row),
        pl.BlockSpec((tm, 512), row),
        pl.BlockSpec((tm, LANES), row),
    ]
    out_shape = [
        jax.ShapeDtypeStruct((bsz, 512, seq), BF16),
        jax.ShapeDtypeStruct((t, LANES), BF16),
        jax.ShapeDtypeStruct((bsz, ATT_KV_HEADS, HEAD_DIM + GQA_ONES_ROWS, seq), BF16),
        jax.ShapeDtypeStruct((t, 512), BF16),
        jax.ShapeDtypeStruct((bsz, 512, seq), BF16),
        jax.ShapeDtypeStruct((t, 512), BF16),
        jax.ShapeDtypeStruct((t, 512), F32),
        jax.ShapeDtypeStruct((t, 512), F32),
        jax.ShapeDtypeStruct((t, 512), F32),
        jax.ShapeDtypeStruct((t, 512), F32),
        jax.ShapeDtypeStruct((t, LANES), F32),
    ]
    return pl.pallas_call(
        functools.partial(_proj_kernel, tm=tm, nts=nts),
        grid=(t // tm,),
        in_specs=in_specs,
        out_specs=out_specs,
        out_shape=out_shape,
        scratch_shapes=[pltpu.VMEM((tm + 2 * HALO, D_MODEL), BF16), pltpu.VMEM((tm + 2 * HALO, 1536), F32)],
        compiler_params=_cparams(("parallel",)),
        name="proj",
    )(x, x, x, cos_t, sin_t, wa, wn, wc, wz, wab, qg, kg, cw, gp)


GQA_TQ = 1024
GQA_QSUB = 128
GQA_ONES_ROWS = 16
GQA_KEY_CHUNK = 1024


def _gqa_kernel(qt_ref, k_ref, vt_ref, o_ref):
    seq = k_ref.shape[1]
    kc = GQA_KEY_CHUNK
    nkc = seq // kc
    qs = GQA_QSUB
    zeros = jnp.zeros((HEAD_DIM, 2 * qs), BF16)
    groups = [(h, 4 * h + 2 * pr, t) for t in range(GQA_TQ // qs) for h in range(ATT_KV_HEADS) for pr in range(2)]

    def scores(h, g0, t):
        cols = slice(qs * t, qs * (t + 1))
        w = jnp.concatenate([qt_ref[0, HEAD_DIM * g0:HEAD_DIM * (g0 + 1), cols],
                             qt_ref[0, HEAD_DIM * (g0 + 1):HEAD_DIM * (g0 + 2), cols]], axis=1)
        w = jnp.concatenate([w, zeros] if h == 0 else [zeros, w], axis=0)
        sts = [jnp.dot(k_ref[0, kc * c:kc * (c + 1), :], w, preferred_element_type=F32) for c in range(nkc)]
        m = functools.reduce(jnp.maximum, [jnp.max(st, axis=0, keepdims=True) for st in sts])
        return sts, m

    def outputs(h, g0, t, sts, m):
        ot = None
        for c in range(nkc):
            pt = jnp.exp2(sts[c] - m).astype(BF16)
            part = jnp.dot(vt_ref[0, h, :, kc * c:kc * (c + 1)], pt, preferred_element_type=F32)
            ot = part if ot is None else ot + part
        on = ot[:HEAD_DIM] * (1.0 / ot[HEAD_DIM:HEAD_DIM + 1])
        both = jnp.concatenate([on[:, :qs], on[:, qs:]], axis=0)
        o_ref[0, qs * t:qs * (t + 1), HEAD_DIM * g0:HEAD_DIM * (g0 + 2)] = both.T.astype(BF16)

    pending = scores(*groups[0])
    for n, grp in enumerate(groups):
        nxt = scores(*groups[n + 1]) if n + 1 < len(groups) else None
        outputs(*grp, *pending)
        pending = nxt


def _gqa_call(qt, ka, vt):
    bsz, _, seq = qt.shape
    return pl.pallas_call(
        _gqa_kernel,
        grid=(bsz, seq // GQA_TQ),
        in_specs=[
            pl.BlockSpec((1, 512, GQA_TQ), lambda b, i: (b, 0, i)),
            pl.BlockSpec((1, seq, LANES), lambda b, i: (b, 0, 0)),
            pl.BlockSpec((1,) + vt.shape[1:], lambda b, i: (b, 0, 0, 0)),
        ],
        out_specs=pl.BlockSpec((1, GQA_TQ, 512), lambda b, i: (b, i, 0)),
        out_shape=jax.ShapeDtypeStruct((bsz, seq, 512), BF16),
        compiler_params=_cparams(("parallel", "arbitrary")),
        name="gqa",
    )(qt, ka, vt)


NA_QBLK = 128


def _na_geometry(seq):
    rows = seq // GRID_W
    wr = min(NA_WIN_ROWS, rows)
    wc = NA_WIN_COLS
    qrows = NA_QBLK // GRID_W
    nblk = seq // NA_QBLK
    nkb = min(5, nblk)
    krows = nkb * qrows
    starts = np.clip(np.arange(nblk) - 2, 0, nblk - nkb)
    variants, var_of = [], []
    for i in range(nblk):
        r = i * qrows + np.arange(qrows)
        r0 = np.clip(r - wr // 2, 0, rows - wr)
        kr = starts[i] * qrows + np.arange(krows)
        valid = (kr[None] >= r0[:, None]) & (kr[None] < r0[:, None] + wr)
        assert (valid.sum(1) == wr).all()
        dr = kr[None] - r[:, None] + (NA_WIN_ROWS - 1)
        sel = (valid[..., None] & (dr[..., None] == np.arange(2 * NA_WIN_ROWS - 1))).astype(np.float32)
        key = sel.tobytes()
        if key not in variants:
            variants.append(key)
        var_of.append(variants.index(key))
    row_sel = np.stack([np.frombuffer(k, np.float32).reshape(qrows, krows, 2 * NA_WIN_ROWS - 1) for k in variants])
    c = np.arange(GRID_W)
    c0 = np.clip(c - wc // 2, 0, GRID_W - wc)
    validc = (c[None] >= c0[:, None]) & (c[None] < c0[:, None] + wc)
    dc = c[None] - c[:, None] + (NA_WIN_COLS - 1)
    col_sel = (validc[..., None] & (dc[..., None] == np.arange(2 * NA_WIN_COLS - 1))).astype(np.float32)
    return nkb, starts.astype(np.int32), np.asarray(var_of, np.int32), row_sel, col_sel


def _na_bias_table(rpb, row_sel, col_sel):
    hi = lax.Precision.HIGHEST
    cols = jnp.einsum('hrd,cnd->hrcn', rpb, col_sel, precision=hi)
    tab = jnp.einsum('vqkr,hrcn->vhqckn', row_sel, cols, precision=hi)
    valid = (row_sel.sum(-1) > 0)[:, None, :, None, :, None] & (col_sel.sum(-1) > 0)[None, None, None, :, None, :]
    tab = jnp.where(valid, tab * LOG2E, NEG)
    nv, h = tab.shape[:2]
    return tab.reshape(nv, h // 2, 2 * NA_QBLK, -1)


def _na_kernel(var_ref, kb_ref, q_ref, *refs, nkb):
    kt_refs = refs[:nkb]
    v_refs = refs[nkb:2 * nkb]
    bias_ref = refs[2 * nkb]
    o_ref = refs[2 * nkb + 1]
    lane = lax.broadcasted_iota(jnp.int32, (NA_QBLK, LANES), 1)
    lo = lane < HEAD_DIM
    def scores(pp):
        sl = slice(LANES * pp, LANES * (pp + 1))
        qp = q_ref[0, :, sl]
        zero = jnp.zeros_like(qp)
        q2 = jnp.concatenate([jnp.where(lo, qp, zero), jnp.where(lo, zero, qp)], axis=0)
        ktp = jnp.concatenate([r[0, sl, :] for r in kt_refs], axis=1)
        return jnp.dot(q2, ktp, preferred_element_type=F32) + bias_ref[0, pp]

    def outputs(pp, s):
        sl = slice(LANES * pp, LANES * (pp + 1))
        vp = jnp.concatenate([r[0, :, sl] for r in v_refs], axis=0)
        m = jnp.max(s, axis=-1, keepdims=True)
        p = jnp.exp2(s - m)
        l = jnp.sum(p, axis=-1, keepdims=True)
        pv = jnp.dot(p.astype(BF16), vp, preferred_element_type=F32) / l
        o_ref[0, :, sl] = jnp.where(lo, pv[:NA_QBLK], pv[NA_QBLK:]).astype(BF16)

    npairs = NA_HEADS // 2
    pending = scores(0)
    for pp in range(npairs):
        nxt = scores(pp + 1) if pp + 1 < npairs else None
        outputs(pp, pending)
        pending = nxt


def _na_call(nq, nkt, nv, rpb):
    bsz, seq, _ = nq.shape
    nkb, starts, var_of, row_sel, col_sel = _na_geometry(seq)
    bias = _na_bias_table(rpb.astype(F32), row_sel, col_sel)
    nblk = seq // NA_QBLK
    in_specs = [pl.BlockSpec((1, NA_QBLK, 512), lambda b, i, var, kb: (b, i, 0))]
    for j in range(nkb):
        in_specs.append(pl.BlockSpec((1, 512, NA_QBLK), lambda b, i, var, kb, j=j: (b, 0, kb[i] + j)))
    for j in range(nkb):
        in_specs.append(pl.BlockSpec((1, NA_QBLK, 512), lambda b, i, var, kb, j=j: (b, kb[i] + j, 0)))
    in_specs.append(pl.BlockSpec((1,) + bias.shape[1:], lambda b, i, var, kb: (var[i], 0, 0, 0)))
    grid_spec = pltpu.PrefetchScalarGridSpec(
        num_scalar_prefetch=2,
        grid=(bsz, nblk),
        in_specs=in_specs,
        out_specs=pl.BlockSpec((1, NA_QBLK, 512), lambda b, i, var, kb: (b, i, 0)),
    )
    return pl.pallas_call(
        functools.partial(_na_kernel, nkb=nkb),
        grid_spec=grid_spec,
        out_shape=jax.ShapeDtypeStruct((bsz, seq, 512), BF16),
        compiler_params=_cparams(("parallel", "arbitrary")),
        name="na",
    )(jnp.asarray(var_of), jnp.asarray(starts), nq, *([nkt] * nkb), *([nv] * nkb), bias)


DN_TC = 512
DN_NHP = 4


def _bd(x, lo):
    xb = x.astype(BF16)
    zero = jnp.zeros_like(xb)
    return jnp.concatenate([jnp.where(lo, xb, zero), jnp.where(lo, zero, xb)], axis=-2)


def _bmm(a, b16):
    return jnp.einsum('bik,bkj->bij', a.astype(BF16), b16, preferred_element_type=F32)


def _dn_prepare(d, hp, q, k, v, gn, gr):
    C = DN_CHUNK
    nch = DN_TC // C
    sgn = 1 - 2 * d

    col = lax.broadcasted_iota(jnp.int32, (LANES, 2 * LANES), 0)
    ln2 = lax.broadcasted_iota(jnp.int32, (LANES, 2 * LANES), 1)
    want = (ln2 // LANES) * 16 + d * 8 + 2 * hp + (ln2 % LANES) // HEAD_DIM
    onehot = jnp.where(col == want, 1.0, 0.0).astype(BF16)
    gb = _dot_exact_rhs(gn, onehot)
    g_col = gb[:, :LANES]
    beta = gb[:, LANES:]
    pos = lax.broadcasted_iota(jnp.int32, (DN_TC, LANES), 0) % C
    gc = g_col
    shift = 1
    while shift < C:
        if d == 0:
            gc = gc + jnp.where(pos >= shift, pltpu.roll(gc, shift, 0), 0.0)
        else:
            gc = gc + jnp.where(pos < C - shift, pltpu.roll(gc, DN_TC - shift, 0), 0.0)
        shift *= 2
    last = C - 1 if d == 0 else 0
    tot = jnp.concatenate([jnp.broadcast_to(gc[C * j + last:C * j + last + 1, :], (C, LANES)) for j in range(nch)], axis=0)
    r2 = lax.broadcasted_iota(jnp.int32, (LANES, LANES), 0)
    c2 = lax.broadcasted_iota(jnp.int32, (LANES, LANES), 1)
    tri2 = jnp.where((r2 // C == c2 // C) & ((c2 - r2) * sgn >= 0), 1.0, 0.0).astype(BF16)
    gcr = _dot_exact_rhs(gr, tri2)

    egc = jnp.exp(gc)
    kbeta = k * beta
    vbeta = v * beta
    kbd = kbeta * egc
    qd = q * egc
    kd = k * jnp.exp(tot - gc)
    etot = jnp.exp(tot)

    row = lax.broadcasted_iota(jnp.int32, (C, LANES), 0)
    jj = lax.broadcasted_iota(jnp.int32, (C, LANES), 1) % HEAD_DIM
    earlier_eq = (row - jj) * sgn >= 0
    strict = jnp.where((row - jj) * sgn > 0, 1.0, 0.0)

    def chunks(a):
        return a.reshape(nch, C, a.shape[-1])

    gc3 = chunks(gc)
    diff = jnp.stack([gc3[j] - gcr[j:j + 1, :] for j in range(nch)])
    decay = jnp.where(earlier_eq, jnp.exp(jnp.minimum(diff, 0.0)), 0.0)
    return dict(q=chunks(q), k=chunks(k), kbeta=chunks(kbeta), vbeta=chunks(vbeta), kbd=chunks(kbd), qd=chunks(qd),
                kd=chunks(kd), etot=chunks(etot), decay=decay, decay_strict=decay * strict)


def _dn_solve(e):
    C = DN_CHUNK
    row = lax.broadcasted_iota(jnp.int32, (C, LANES), 0)
    lane = lax.broadcasted_iota(jnp.int32, (C, LANES), 1)
    lo = lane < HEAD_DIM
    eye2 = jnp.where(row == lane % HEAD_DIM, 1.0, 0.0)
    ai = jnp.einsum('bik,bjk->bij', jnp.concatenate([e['kbeta'], e['q']], axis=1).astype(BF16), _bd(e['k'], lo),
                    preferred_element_type=F32)
    intra = ai[:, C:] * e['decay']
    x = -ai[:, :C] * e['decay_strict']
    t = eye2 + x
    p = _bmm(x, _bd(x, lo))
    for lvl in range(5):
        pbd = _bd(p, lo)
        if lvl < 4:
            rr = _bmm(jnp.concatenate([p, t], axis=1), pbd)
            p = rr[:, :C]
            t = t + rr[:, C:]
        else:
            t = t + _bmm(t, pbd)
    uw = _bmm(t, jnp.concatenate([_bd(e['vbeta'], lo), _bd(e['kbd'], lo)], axis=2))
    kdt = jnp.stack([e['kd'][b].T for b in range(e['kd'].shape[0])]).astype(BF16)
    r3 = lax.broadcasted_iota(jnp.int32, (LANES, 2 * LANES), 0) // HEAD_DIM
    c3 = (lax.broadcasted_iota(jnp.int32, (LANES, 2 * LANES), 1) % LANES) // HEAD_DIM
    nm = jnp.where(r3 == c3, _bmm(kdt, uw.astype(BF16)), 0.0)
    return uw, intra, nm


def _dn_kernel(qf_ref, kf_ref, vf_ref, gnf_ref, grf_ref, qb_ref, kb_ref, vb_ref, gnb_ref, grb_ref,
               of_ref, ob_ref, s_ref):
    C = DN_CHUNK
    nch = DN_TC // C

    @pl.when(pl.program_id(2) == 0)
    def _():
        s_ref[...] = jnp.zeros_like(s_ref)

    in_refs = ((qf_ref, kf_ref, vf_ref, gnf_ref, grf_ref), (qb_ref, kb_ref, vb_ref, gnb_ref, grb_ref))
    out_refs = (of_ref, ob_ref)
    lo = lax.broadcasted_iota(jnp.int32, (C, LANES), 1) < HEAD_DIM

    chains = [(d, i) for d in range(2) for i in range(DN_NHP)]
    per = []
    for d, i in chains:
        q_ref, k_ref, v_ref, gn_ref, gr_ref = in_refs[d]
        sl = slice(LANES * i, LANES * (i + 1))
        hp = pl.program_id(1) * DN_NHP + i
        per.append(_dn_prepare(d, hp, q_ref[0, :, sl], k_ref[0, :, sl], v_ref[0, :, sl], gn_ref[0], gr_ref[0, i, 0]))
    e = {name: jnp.concatenate([p[name] for p in per], axis=0) for name in per[0]}
    uw, intra, nm = _dn_solve(e)
    qd, etot = e['qd'], e['etot']
    states = [s_ref[d, i] for d, i in chains]
    for step in range(nch):
        for n, (d, i) in enumerate(chains):
            j = step if d == 0 else nch - 1 - step
            b = n * nch + j
            s16 = states[n].astype(BF16)
            states[n] = (states[n] * etot[b, 0:1] + nm[b, :, :LANES]
                         - jnp.dot(nm[b, :, LANES:].astype(BF16), s16, preferred_element_type=F32))
            wq = jnp.dot(jnp.concatenate([uw[b, :, LANES:], qd[b]], axis=0).astype(BF16), s16,
                         preferred_element_type=F32)
            v_new = uw[b, :, :LANES] - wq[:C]
            out_refs[d][0, C * j:C * (j + 1), LANES * i:LANES * (i + 1)] = wq[C:] + jnp.dot(
                intra[b].astype(BF16), _bd(v_new, lo), preferred_element_type=F32)
    for n, (d, i) in enumerate(chains):
        s_ref[d, i] = states[n]


def _dn_call(dq, dk, dv, gn, gr):
    bsz, seq, _ = dq.shape
    nsc = seq // DN_TC
    nch = DN_TC // DN_CHUNK
    width = LANES * DN_NHP
    in_specs = []
    for d in range(2):
        tmap = (lambda c: c) if d == 0 else (lambda c: nsc - 1 - c)
        qkv_spec = pl.BlockSpec((1, DN_TC, width), lambda b, hp, c, tmap=tmap: (b, tmap(c), hp))
        in_specs += [
            qkv_spec, qkv_spec, qkv_spec,
            pl.BlockSpec((1, DN_TC, LANES), lambda b, hp, c, tmap=tmap: (b, tmap(c), 0)),
            pl.BlockSpec((1, DN_NHP, 1, nch, LANES), lambda b, hp, c, tmap=tmap, d=d: (d, hp, b, tmap(c), 0)),
        ]
    out_specs = [
        pl.BlockSpec((1, DN_TC, width), lambda b, hp, c: (b, c, hp)),
        pl.BlockSpec((1, DN_TC, width), lambda b, hp, c: (b, nsc - 1 - c, hp)),
    ]
    return pl.pallas_call(
        _dn_kernel,
        grid=(bsz, DN_HEADS // 2 // DN_NHP, nsc),
        in_specs=in_specs,
        out_specs=out_specs,
        out_shape=[jax.ShapeDtypeStruct((bsz, seq, 512), F32)] * 2,
        scratch_shapes=[pltpu.VMEM((2, DN_NHP, LANES, LANES), F32)],
        compiler_params=_cparams(("parallel", "parallel", "arbitrary")),
        name="deltanet",
    )(dq, dk, dv, gn, gr, dq, dk, dv, gn, gr)


def _dn_rows(gn, bsz, seq):
    g = gn[:, :32].reshape(bsz, seq // DN_CHUNK, DN_CHUNK, 2, 2, DN_HEADS // 2, 2)
    g = g[:, :, :, 0]
    g = jnp.transpose(g, (3, 4, 0, 1, 5, 2))
    return g.reshape(2, DN_HEADS // 2, bsz, seq // DN_CHUNK, LANES)


def _merge_kernel(x_ref, bra_ref, brb_ref, of_ref, ob_ref, cz_ref, wg_ref, bg_ref, wb_ref, wo_ref,
                  ng_ref, g1_ref, b1_ref, o_ref):
    x = x_ref[...]
    xb = x.astype(BF16)
    ones = _seg_ones(LANES, HEAD_DIM)
    parts = []
    for j in range(4):
        sl = slice(LANES * j, LANES * (j + 1))
        o = of_ref[:, sl] + ob_ref[:, sl]
        ss = _seg_sum(o * o, ones)
        parts.append((o * lax.rsqrt(ss * (1.0 / HEAD_DIM) + RMS_EPS) * ng_ref[...] * _silu(cz_ref[:, sl])).astype(BF16))
    brc = jnp.concatenate(parts, axis=1)
    merged = None
    for n, br in enumerate((bra_ref[...], brb_ref[...], brc)):
        sl = slice(D_MODEL * n, D_MODEL * (n + 1))
        gate = jax.nn.sigmoid(jnp.dot(xb, wg_ref[:, sl], preferred_element_type=F32) + bg_ref[:, sl])
        term = gate * jnp.dot(br, wb_ref[n], preferred_element_type=F32)
        merged = term if merged is None else merged + term
    y = DEEPNORM_ALPHA * x + jnp.dot(merged.astype(BF16), wo_ref[...], preferred_element_type=F32)
    o_ref[...] = _layer_norm(y, g1_ref[...], b1_ref[...])


def _merge_call(x, bra, brb, o_f, o_b, cz, wg, bg, wb, wo, ng, g1, b1, tm=512):
    t = x.shape[0]
    row = lambda i: (i, 0)
    return pl.pallas_call(
        _merge_kernel,
        grid=(t // tm,),
        in_specs=[
            pl.BlockSpec((tm, D_MODEL), row),
            pl.BlockSpec((tm, 512), row),
            pl.BlockSpec((tm, 512), row),
            pl.BlockSpec((tm, 512), row),
            pl.BlockSpec((tm, 512), row),
            pl.BlockSpec((tm, 512), row),
            _const_spec(wg.shape), _const_spec(bg.shape), _const_spec(wb.shape), _const_spec(wo.shape),
            _const_spec(ng.shape), _const_spec(g1.shape), _const_spec(b1.shape),
        ],
        out_specs=pl.BlockSpec((tm, D_MODEL), row),
        out_shape=jax.ShapeDtypeStruct((t, D_MODEL), F32),
        compiler_params=_cparams(("parallel",)),
        name="merge",
    )(x, bra, brb, o_f, o_b, cz, wg, bg, wb, wo, ng, g1, b1)


def _ffn_kernel(x_ref, pa_ref, pb_ref, wfg_ref, wfu_ref, wfo_ref, wpg_ref, bpg_ref, wpp_ref, g2_ref, b2_ref, o_ref,
                *, n_first):
    x = x_ref[...]
    xb = x.astype(BF16)
    p = jnp.where(pl.program_id(0) < n_first, pa_ref[0], pb_ref[0])
    gate = jnp.dot(xb, wfg_ref[...], preferred_element_type=F32)
    up = jnp.dot(xb, wfu_ref[...], preferred_element_type=F32)
    ffn = jnp.dot((_silu(gate) * up).astype(BF16), wfo_ref[...], preferred_element_type=F32)
    ple = (jax.nn.sigmoid(jnp.dot(xb, wpg_ref[...], preferred_element_type=F32) + bpg_ref[...])
           * jnp.dot(p.astype(BF16), wpp_ref[...], preferred_element_type=F32))
    o_ref[...] = _layer_norm(DEEPNORM_ALPHA * x + ffn + ple, g2_ref[...], b2_ref[...])


def _ffn_call(x, pa, pb, layer, wfg, wfu, wfo, wpg, bpg, wpp, g2, b2, tm=256):
    t = x.shape[0]
    n_first = pa.shape[1] // tm
    row = lambda i: (i, 0)
    return pl.pallas_call(
        functools.partial(_ffn_kernel, n_first=n_first),
        grid=(t // tm,),
        in_specs=[
            pl.BlockSpec((tm, D_MODEL), row),
            pl.BlockSpec((1, tm, PLE_DIM), lambda i: (layer, jnp.minimum(i, n_first - 1), 0)),
            pl.BlockSpec((1, tm, PLE_DIM), lambda i: (layer, jnp.maximum(i - n_first, 0), 0)),
            _const_spec(wfg.shape), _const_spec(wfu.shape), _const_spec(wfo.shape), _const_spec(wpg.shape),
            _const_spec(bpg.shape), _const_spec(wpp.shape), _const_spec(g2.shape), _const_spec(b2.shape),
        ],
        out_specs=pl.BlockSpec((tm, D_MODEL), row),
        out_shape=jax.ShapeDtypeStruct((t, D_MODEL), F32),
        compiler_params=_cparams(("arbitrary",)),
        name="ffn",
    )(x, pa, pb, wfg, wfu, wfo, wpg, bpg, wpp, g2, b2)


def _rope_tables(seq):
    t = jnp.arange(seq, dtype=jnp.int32)
    rowf = (t // GRID_W).astype(F32)
    colf = (t % GRID_W).astype(F32)
    inv_freq = ROPE_THETA ** (-jnp.arange(ROPE_PAIRS, dtype=F32) / ROPE_PAIRS)
    ang_r = rowf[:, None] * inv_freq[None, :]
    ang_c = colf[:, None] * inv_freq[None, :]
    cos = jnp.concatenate([jnp.cos(ang_r), jnp.cos(ang_r), jnp.cos(ang_c), jnp.cos(ang_c)], axis=1)
    sin = jnp.concatenate([-jnp.sin(ang_r), jnp.sin(ang_r), -jnp.sin(ang_c), jnp.sin(ang_c)], axis=1)
    return jnp.tile(cos, (1, 2)), jnp.tile(sin, (1, 2))


def _tile_lanes(v, reps):
    return jnp.tile(v.astype(F32), reps).reshape(1, -1)


def _layer(x, pa, pb, layer, bsz, seq, tables, w):
    (w_in, qn_g, kn_g, rpb, conv_w, a_log, dt_bias, dn_g, w_gate, b_gate, w_branch, w_out, ln1_g, ln1_b,
     w_ffn_in, w_ffn_out, w_ple_gate, b_ple_gate, w_ple_proj, ln2_g, ln2_b) = w
    o = _OFF
    wa = w_in[:, o[0]:o[3]].astype(BF16)
    wn = w_in[:, o[3]:o[6]].astype(BF16)
    wc = w_in[:, o[6]:o[9]].astype(BF16)
    wz = w_in[:, o[9]:o[10]].astype(BF16)
    wab = jnp.pad(w_in[:, o[10]:o[14]], ((0, 0), (0, LANES - 32))).astype(BF16)
    gp = jnp.zeros((8, LANES), F32)
    gp = gp.at[0, :16].set(a_log.reshape(16).astype(F32))
    gp = gp.at[1, :16].set(dt_bias.reshape(16).astype(F32))
    gp = gp.at[2, :16].set(1.0)
    cos_t, sin_t = tables
    (qt, ka, vt, nq, nkt, nv, dq, dk, dv, cz, gn) = _proj_call(
        x, bsz, seq, cos_t, sin_t, wa, wn, wc, wz, wab,
        _tile_lanes(qn_g, 2), _tile_lanes(kn_g, 2), conv_w.astype(F32), gp)
    r3 = lambda a: a.reshape(bsz, seq, a.shape[-1])
    bra = _gqa_call(qt, r3(ka), vt)
    brb = _na_call(r3(nq), nkt, r3(nv), rpb)
    o_f, o_b = _dn_call(r3(dq), r3(dk), r3(dv), r3(gn), _dn_rows(gn, bsz, seq))
    t = bsz * seq
    x = _merge_call(x, bra.reshape(t, 512), brb.reshape(t, 512), o_f.reshape(t, 512), o_b.reshape(t, 512), cz,
                    w_gate.astype(BF16), b_gate.astype(F32).reshape(1, -1), w_branch.astype(BF16),
                    w_out.astype(BF16), _tile_lanes(dn_g, 2), ln1_g.astype(F32).reshape(1, -1),
                    ln1_b.astype(F32).reshape(1, -1))
    x = _ffn_call(x, pa, pb, layer, w_ffn_in[:, :FFN_HIDDEN].astype(BF16), w_ffn_in[:, FFN_HIDDEN:].astype(BF16),
                  w_ffn_out.astype(BF16), w_ple_gate.astype(BF16), b_ple_gate.astype(F32).reshape(1, -1),
                  w_ple_proj.astype(BF16), ln2_g.astype(F32).reshape(1, -1), ln2_b.astype(F32).reshape(1, -1))
    return x


def _trunk(xa, xb, pa, pb, emb_ln_g, emb_ln_b, weights):
    (na, seq, d), nb = xa.shape, xb.shape[0]
    depth = pa.shape[0]
    bsz = na + nb
    xf = _ln_call(xa.reshape(na * seq, d), xb.reshape(nb * seq, d), emb_ln_g.astype(F32), emb_ln_b.astype(F32))
    pa = pa.reshape(depth, na * seq, -1)
    pb = pb.reshape(depth, nb * seq, -1)
    tables = _rope_tables(seq)
    for i in range(depth):
        xf = _layer(xf, pa, pb, i, bsz, seq, tables, tuple(a[i] for a in weights))
    y = xf.reshape(bsz, seq, d)
    return y[:na], y[na:]


def kernel(x_prompt, x_sample, p_prompt, p_sample, emb_ln_g, emb_ln_b, w_in, att_q_norm_g, att_k_norm_g, na_rpb, dn_conv_w, dn_a_log, dn_dt_bias, dn_norm_g, w_gate, b_gate, w_branch, w_out, ln1_g, ln1_b, w_ffn_in, w_ffn_out, w_ple_gate, b_ple_gate, w_ple_proj, ln2_g, ln2_b):
    weights = (w_in, att_q_norm_g, att_k_norm_g, na_rpb, dn_conv_w, dn_a_log, dn_dt_bias, dn_norm_g, w_gate,
               b_gate, w_branch, w_out, ln1_g, ln1_b, w_ffn_in, w_ffn_out, w_ple_gate, b_ple_gate, w_ple_proj,
               ln2_g, ln2_b)
    return _trunk(x_prompt, x_sample, p_prompt, p_sample, emb_ln_g, emb_ln_b, weights)
```

```python
import functools

import numpy as np
import jax
import jax.numpy as jnp
from jax import lax
from jax.experimental import pallas as pl
from jax.experimental.pallas import tpu as pltpu

F32 = jnp.float32
BF16 = jnp.bfloat16

D_MODEL = 1024
DEPTH = 4
GRID_W = 64
HEAD_DIM = 64
ATT_Q_HEADS = 8
ATT_KV_HEADS = 2
ROPE_PAIRS = HEAD_DIM // 4
ROPE_THETA = 10000.0
NA_HEADS = 8
NA_WIN_ROWS = 8
NA_WIN_COLS = 16
DN_HEADS = 8
DN_CHUNK = 64
BRANCH_WIDTH = 512
FFN_HIDDEN = 2816
PLE_DIM = 256
LN_EPS = 1e-5
RMS_EPS = 1e-6
L2_EPS = 1e-6
DEEPNORM_ALPHA = (2 * DEPTH) ** 0.25

LANES = 128
HALO = 16
VMEM_LIMIT = 56 * 1024 * 1024
NEG = -1e30
LOG2E = 1.4426950408889634

_OFF = np.cumsum([0, 512, 128, 128, 512, 512, 512, 512, 512, 512, 512, 8, 8, 8, 8])


def _cparams(sem):
    return pltpu.CompilerParams(dimension_semantics=sem, vmem_limit_bytes=VMEM_LIMIT)


def _const_spec(shape):
    nd = len(shape)
    return pl.BlockSpec(shape, lambda *a: (0,) * nd, pipeline_mode=pl.Buffered(1))


def _bdot(a, b):
    return jnp.dot(a.astype(BF16), b.astype(BF16), preferred_element_type=F32)


def _split(x, n):
    out = []
    r = x
    for _ in range(n):
        p = r.astype(BF16)
        out.append(p)
        r = r - p.astype(F32)
    return out


def _dot_exact_rhs(a, b01, n=3):
    acc = None
    for p in _split(a, n):
        t = jnp.dot(p, b01, preferred_element_type=F32)
        acc = t if acc is None else acc + t
    return acc


def _dot_exact_lhs(a01, b, n=3):
    acc = None
    for p in _split(b, n):
        t = jnp.dot(a01, p, preferred_element_type=F32)
        acc = t if acc is None else acc + t
    return acc


def _seg_ones(n, seg):
    r = lax.broadcasted_iota(jnp.int32, (n, n), 0) // seg
    c = lax.broadcasted_iota(jnp.int32, (n, n), 1) // seg
    return jnp.where(r == c, 1.0, 0.0).astype(BF16)


def _seg_sum(x, ones):
    return _dot_exact_rhs(x, ones, 1)


def _layer_norm(y, g, b):
    mu = jnp.mean(y, axis=-1, keepdims=True)
    yc = y - mu
    var = jnp.mean(yc * yc, axis=-1, keepdims=True)
    return yc * lax.rsqrt(var + LN_EPS) * g + b


def _silu(x):
    return x * jax.nn.sigmoid(x)


def _two_group_specs(block, n_first):
    nd = len(block)
    tail = (0,) * (nd - 1)
    return [pl.BlockSpec(block, lambda i: (jnp.minimum(i, n_first - 1),) + tail),
            pl.BlockSpec(block, lambda i: (jnp.maximum(i - n_first, 0),) + tail)]


def _ln_kernel(xa_ref, xb_ref, g_ref, b_ref, o_ref, *, n_first):
    x = jnp.where(pl.program_id(0) < n_first, xa_ref[...], xb_ref[...])
    o_ref[...] = _layer_norm(x, g_ref[...], b_ref[...])


def _ln_call(xa, xb, g, b, tm=1024):
    d = xa.shape[1]
    t = xa.shape[0] + xb.shape[0]
    n_first = xa.shape[0] // tm
    return pl.pallas_call(
        functools.partial(_ln_kernel, n_first=n_first),
        grid=(t // tm,),
        in_specs=_two_group_specs((tm, d), n_first) + [_const_spec((1, d)), _const_spec((1, d))],
        out_specs=pl.BlockSpec((tm, d), lambda i: (i, 0)),
        out_shape=jax.ShapeDtypeStruct((t, d), F32),
        compiler_params=_cparams(("arbitrary",)),
        name="emb_ln",
    )(xa, xb, g.reshape(1, d), b.reshape(1, d))


def _proj_kernel(xp_ref, x_ref, xn_ref, cos_ref, sin_ref, wa_ref, wn_ref, wc_ref, wz_ref, wab_ref,
                 qg_ref, kg_ref, cw_ref, gp_ref,
                 qt_ref, ka_ref, vt_ref, nq_ref, nkt_ref, nv_ref, dq_ref, dk_ref, dv_ref, cz_ref, gn_ref,
                 xs_ref, c_ref, *, tm, nts):
    i = pl.program_id(0)
    first = (i % nts) == 0
    last = (i % nts) == nts - 1
    xs_ref[0:HALO, :] = jnp.where(first, 0.0, xp_ref[...]).astype(BF16)
    xs_ref[HALO:HALO + tm, :] = x_ref[...].astype(BF16)
    xs_ref[HALO + tm:, :] = jnp.where(last, 0.0, xn_ref[...]).astype(BF16)
    xm = xs_ref[HALO:HALO + tm, :]

    ones = _seg_ones(LANES, HEAD_DIM)
    cos = cos_ref[...]
    sin = sin_ref[...]
    lane = lax.broadcasted_iota(jnp.int32, (tm, LANES), 1)
    first_half = (lane % (2 * ROPE_PAIRS)) < ROPE_PAIRS

    def norm_rope(xq, g):
        ss = _seg_sum(xq * xq, ones)
        xn = xq * lax.rsqrt(ss * (1.0 / HEAD_DIM) + RMS_EPS) * g
        partner = jnp.where(first_half, pltpu.roll(xn, LANES - ROPE_PAIRS, 1), pltpu.roll(xn, ROPE_PAIRS, 1))
        return xn * cos + partner * sin

    scale = HEAD_DIM ** -0.5
    scale2 = scale * LOG2E

    c_ref[...] = jnp.dot(xs_ref[...], wc_ref[...], preferred_element_type=F32)

    def conv_group(j):
        sl = slice(LANES * j, LANES * (j + 1))
        y = (c_ref[HALO - 1:HALO - 1 + tm, sl] * cw_ref[0:1, sl]
             + c_ref[HALO:HALO + tm, sl] * cw_ref[1:2, sl]
             + c_ref[HALO + 1:HALO + 1 + tm, sl] * cw_ref[2:3, sl])
        y = _silu(y)
        if j < 8:
            y = y * lax.rsqrt(_seg_sum(y * y, ones) + L2_EPS)
        if j < 4:
            y = y * scale
        osl = slice(LANES * (j % 4), LANES * (j % 4 + 1))
        (dq_ref, dk_ref, dv_ref)[j // 4][:, osl] = y

    ya = jnp.dot(xm, wa_ref[...], preferred_element_type=F32)
    for j in range(0, 4):
        conv_group(j)
    for j in range(4):
        sl = slice(LANES * j, LANES * (j + 1))
        qt_ref[0, sl, :] = (norm_rope(ya[:, sl], qg_ref[...]) * scale2).T.astype(BF16)
    ka_ref[...] = norm_rope(ya[:, 512:640], kg_ref[...]).astype(BF16)
    vt = ya[:, 640:768].T.astype(BF16)
    for h in range(ATT_KV_HEADS):
        vt_ref[0, h, 0:HEAD_DIM, :] = vt[HEAD_DIM * h:HEAD_DIM * (h + 1)]
        vt_ref[0, h, HEAD_DIM:, :] = jnp.ones((GQA_ONES_ROWS, tm), BF16)

    yn = jnp.dot(xm, wn_ref[...], preferred_element_type=F32)
    for j in range(4, 8):
        conv_group(j)
    nq_ref[...] = (yn[:, 0:512] * scale2).astype(BF16)
    nkt_ref[0] = yn[:, 512:1024].T.astype(BF16)
    nv_ref[...] = yn[:, 1024:1536].astype(BF16)

    cz = jnp.dot(xm, wz_ref[...], preferred_element_type=F32)
    raw = jnp.dot(xm, wab_ref[...], preferred_element_type=F32)
    for j in range(8, 12):
        conv_group(j)
    cz_ref[...] = cz
    a_log = gp_ref[0:1, :]
    dt_b = gp_ref[1:2, :]
    is_g = gp_ref[2:3, :] > 0.5
    z = raw + dt_b
    softplus = jnp.maximum(z, 0.0) + jnp.log(1.0 + jnp.exp(-jnp.abs(z)))
    gn_ref[...] = jnp.where(is_g, -jnp.exp(a_log) * softplus, jax.nn.sigmoid(raw))


def _proj_call(x, bsz, seq, cos_t, sin_t, wa, wn, wc, wz, wab, qg, kg, cw, gp, tm=512):
    t = x.shape[0]
    nts = seq // tm
    nh = t // HALO
    row = lambda i: (i, 0)
    seqrow = lambda i: (i % nts, 0)
    in_specs = [
        pl.BlockSpec((HALO, D_MODEL), lambda i: (jnp.maximum(i * (tm // HALO) - 1, 0), 0)),
        pl.BlockSpec((tm, D_MODEL), row),
        pl.BlockSpec((HALO, D_MODEL), lambda i: (jnp.minimum((i + 1) * (tm // HALO), nh - 1), 0)),
        pl.BlockSpec((tm, LANES), seqrow),
        pl.BlockSpec((tm, LANES), seqrow),
        _const_spec(wa.shape), _const_spec(wn.shape), _const_spec(wc.shape), _const_spec(wz.shape),
        _const_spec(wab.shape), _const_spec(qg.shape), _const_spec(kg.shape), _const_spec(cw.shape),
        _const_spec(gp.shape),
    ]
    tposed = lambda i: (i // nts, 0, i % nts)
    out_specs = [
        pl.BlockSpec((1, 512, tm), tposed),
        pl.BlockSpec((tm, LANES), row),
        pl.BlockSpec((1, ATT_KV_HEADS, HEAD_DIM + GQA_ONES_ROWS, tm), lambda i: (i // nts, 0, 0, i % nts)),
        pl.BlockSpec((tm, 512), row),
        pl.BlockSpec((1, 512, tm), tposed),
        pl.BlockSpec((tm, 512), row),
        pl.BlockSpec((tm, 512), row),
        pl.BlockSpec((tm, 512), row),
        pl.BlockSpec((tm, 512), row),
        pl.BlockSpec((tm, 512), row),
        pl.BlockSpec((tm, LANES), row),
    ]
    out_shape = [
        jax.ShapeDtypeStruct((bsz, 512, seq), BF16),
        jax.ShapeDtypeStruct((t, LANES), BF16),
        jax.ShapeDtypeStruct((bsz, ATT_KV_HEADS, HEAD_DIM + GQA_ONES_ROWS, seq), BF16),
        jax.ShapeDtypeStruct((t, 512), BF16),
        jax.ShapeDtypeStruct((bsz, 512, seq), BF16),
        jax.ShapeDtypeStruct((t, 512), BF16),
        jax.ShapeDtypeStruct((t, 512), F32),
        jax.ShapeDtypeStruct((t, 512), F32),
        jax.ShapeDtypeStruct((t, 512), F32),
        jax.ShapeDtypeStruct((t, 512), F32),
        jax.ShapeDtypeStruct((t, LANES), F32),
    ]
    return pl.pallas_call(
        functools.partial(_proj_kernel, tm=tm, nts=nts),
        grid=(t // tm,),
        in_specs=in_specs,
        out_specs=out_specs,
        out_shape=out_shape,
        scratch_shapes=[pltpu.VMEM((tm + 2 * HALO, D_MODEL), BF16), pltpu.VMEM((tm + 2 * HALO, 1536), F32)],
        compiler_params=_cparams(("parallel",)),
        name="proj",
    )(x, x, x, cos_t, sin_t, wa, wn, wc, wz, wab, qg, kg, cw, gp)


GQA_TQ = 512
GQA_QSUB = 128
GQA_ONES_ROWS = 16
GQA_KEY_CHUNK = 1024


def _gqa_kernel(qt_ref, k_ref, vt_ref, o_ref):
    seq = k_ref.shape[1]
    kc = GQA_KEY_CHUNK
    nkc = seq // kc
    qs = GQA_QSUB
    zeros = jnp.zeros((HEAD_DIM, 2 * qs), BF16)
    groups = [(h, 4 * h + 2 * pr, t) for t in range(GQA_TQ // qs) for h in range(ATT_KV_HEADS) for pr in range(2)]

    def scores(h, g0, t):
        cols = slice(qs * t, qs * (t + 1))
        w = jnp.concatenate([qt_ref[0, HEAD_DIM * g0:HEAD_DIM * (g0 + 1), cols],
                             qt_ref[0, HEAD_DIM * (g0 + 1):HEAD_DIM * (g0 + 2), cols]], axis=1)
        w = jnp.concatenate([w, zeros] if h == 0 else [zeros, w], axis=0)
        sts = [jnp.dot(k_ref[0, kc * c:kc * (c + 1), :], w, preferred_element_type=F32) for c in range(nkc)]
        m = functools.reduce(jnp.maximum, [jnp.max(st, axis=0, keepdims=True) for st in sts])
        return sts, m

    def outputs(h, g0, t, sts, m):
        ot = None
        for c in range(nkc):
            pt = jnp.exp2((sts[c] - m).astype(BF16))
            part = jnp.dot(vt_ref[0, h, :, kc * c:kc * (c + 1)], pt, preferred_element_type=F32)
            ot = part if ot is None else ot + part
        on = ot[:HEAD_DIM] * (1.0 / ot[HEAD_DIM:HEAD_DIM + 1])
        both = jnp.concatenate([on[:, :qs], on[:, qs:]], axis=0)
        o_ref[0, qs * t:qs * (t + 1), HEAD_DIM * g0:HEAD_DIM * (g0 + 2)] = both.T.astype(BF16)

    pending = scores(*groups[0])
    for n, grp in enumerate(groups):
        nxt = scores(*groups[n + 1]) if n + 1 < len(groups) else None
        outputs(*grp, *pending)
        pending = nxt


def _gqa_call(qt, ka, vt):
    bsz, _, seq = qt.shape
    return pl.pallas_call(
        _gqa_kernel,
        grid=(bsz, seq // GQA_TQ),
        in_specs=[
            pl.BlockSpec((1, 512, GQA_TQ), lambda b, i: (b, 0, i)),
            pl.BlockSpec((1, seq, LANES), lambda b, i: (b, 0, 0)),
            pl.BlockSpec((1,) + vt.shape[1:], lambda b, i: (b, 0, 0, 0)),
        ],
        out_specs=pl.BlockSpec((1, GQA_TQ, 512), lambda b, i: (b, i, 0)),
        out_shape=jax.ShapeDtypeStruct((bsz, seq, 512), BF16),
        compiler_params=_cparams(("parallel", "arbitrary")),
        name="gqa",
    )(qt, ka, vt)


NA_QBLK = 128


def _na_geometry(seq):
    rows = seq // GRID_W
    wr = min(NA_WIN_ROWS, rows)
    wc = NA_WIN_COLS
    qrows = NA_QBLK // GRID_W
    nblk = seq // NA_QBLK
    nkb = min(5, nblk)
    krows = nkb * qrows
    starts = np.clip(np.arange(nblk) - 2, 0, nblk - nkb)
    variants, var_of = [], []
    for i in range(nblk):
        r = i * qrows + np.arange(qrows)
        r0 = np.clip(r - wr // 2, 0, rows - wr)
        kr = starts[i] * qrows + np.arange(krows)
        valid = (kr[None] >= r0[:, None]) & (kr[None] < r0[:, None] + wr)
        assert (valid.sum(1) == wr).all()
        dr = kr[None] - r[:, None] + (NA_WIN_ROWS - 1)
        sel = (valid[..., None] & (dr[..., None] == np.arange(2 * NA_WIN_ROWS - 1))).astype(np.float32)
        key = sel.tobytes()
        if key not in variants:
            variants.append(key)
        var_of.append(variants.index(key))
    row_sel = np.stack([np.frombuffer(k, np.float32).reshape(qrows, krows, 2 * NA_WIN_ROWS - 1) for k in variants])
    c = np.arange(GRID_W)
    c0 = np.clip(c - wc // 2, 0, GRID_W - wc)
    validc = (c[None] >= c0[:, None]) & (c[None] < c0[:, None] + wc)
    dc = c[None] - c[:, None] + (NA_WIN_COLS - 1)
    col_sel = (validc[..., None] & (dc[..., None] == np.arange(2 * NA_WIN_COLS - 1))).astype(np.float32)
    return nkb, starts.astype(np.int32), np.asarray(var_of, np.int32), row_sel, col_sel


def _na_bias_table(rpb, row_sel, col_sel):
    hi = lax.Precision.HIGHEST
    cols = jnp.einsum('hrd,cnd->hrcn', rpb, col_sel, precision=hi)
    tab = jnp.einsum('vqkr,hrcn->vhqckn', row_sel, cols, precision=hi)
    valid = (row_sel.sum(-1) > 0)[:, None, :, None, :, None] & (col_sel.sum(-1) > 0)[None, None, None, :, None, :]
    tab = jnp.where(valid, tab * LOG2E, NEG)
    nv, h = tab.shape[:2]
    return tab.reshape(nv, h // 2, 2 * NA_QBLK, -1)


def _na_kernel(var_ref, kb_ref, q_ref, *refs, nkb):
    kt_refs = refs[:nkb]
    v_refs = refs[nkb:2 * nkb]
    bias_ref = refs[2 * nkb]
    o_ref = refs[2 * nkb + 1]
    lane = lax.broadcasted_iota(jnp.int32, (NA_QBLK, LANES), 1)
    lo = lane < HEAD_DIM
    def scores(pp):
        sl = slice(LANES * pp, LANES * (pp + 1))
        qp = q_ref[0, :, sl]
        zero = jnp.zeros_like(qp)
        q2 = jnp.concatenate([jnp.where(lo, qp, zero), jnp.where(lo, zero, qp)], axis=0)
        ktp = jnp.concatenate([r[0, sl, :] for r in kt_refs], axis=1)
        return jnp.dot(q2, ktp, preferred_element_type=F32) + bias_ref[0, pp]

    def outputs(pp, s):
        sl = slice(LANES * pp, LANES * (pp + 1))
        vp = jnp.concatenate([r[0, :, sl] for r in v_refs], axis=0)
        m = jnp.max(s, axis=-1, keepdims=True)
        p = jnp.exp2(s - m)
        l = jnp.sum(p, axis=-1, keepdims=True)
        pv = jnp.dot(p.astype(BF16), vp, preferred_element_type=F32) / l
        o_ref[0, :, sl] = jnp.where(lo, pv[:NA_QBLK], pv[NA_QBLK:]).astype(BF16)

    npairs = NA_HEADS // 2
    pending = scores(0)
    for pp in range(npairs):
        nxt = scores(pp + 1) if pp + 1 < npairs else None
        outputs(pp, pending)
        pending = nxt


def _na_call(nq, nkt, nv, rpb):
    bsz, seq, _ = nq.shape
    nkb, starts, var_of, row_sel, col_sel = _na_geometry(seq)
    bias = _na_bias_table(rpb.astype(F32), row_sel, col_sel)
    nblk = seq // NA_QBLK
    in_specs = [pl.BlockSpec((1, NA_QBLK, 512), lambda b, i, var, kb: (b, i, 0))]
    for j in range(nkb):
        in_specs.append(pl.BlockSpec((1, 512, NA_QBLK), lambda b, i, var, kb, j=j: (b, 0, kb[i] + j)))
    for j in range(nkb):
        in_specs.append(pl.BlockSpec((1, NA_QBLK, 512), lambda b, i, var, kb, j=j: (b, kb[i] + j, 0)))
    in_specs.append(pl.BlockSpec((1,) + bias.shape[1:], lambda b, i, var, kb: (var[i], 0, 0, 0)))
    grid_spec = pltpu.PrefetchScalarGridSpec(
        num_scalar_prefetch=2,
        grid=(bsz, nblk),
        in_specs=in_specs,
        out_specs=pl.BlockSpec((1, NA_QBLK, 512), lambda b, i, var, kb: (b, i, 0)),
    )
    return pl.pallas_call(
        functools.partial(_na_kernel, nkb=nkb),
        grid_spec=grid_spec,
        out_shape=jax.ShapeDtypeStruct((bsz, seq, 512), BF16),
        compiler_params=_cparams(("parallel", "arbitrary")),
        name="na",
    )(jnp.asarray(var_of), jnp.asarray(starts), nq, *([nkt] * nkb), *([nv] * nkb), bias)


DN_TC = 512
DN_NHP = 4


def _bd(x, lo):
    xb = x.astype(BF16)
    zero = jnp.zeros_like(xb)
    return jnp.concatenate([jnp.where(lo, xb, zero), jnp.where(lo, zero, xb)], axis=-2)


def _bmm(a, b16):
    return jnp.einsum('bik,bkj->bij', a.astype(BF16), b16, preferred_element_type=F32)


def _dn_prepare(d, hp, q, k, v, gn, gr):
    C = DN_CHUNK
    nch = DN_TC // C
    sgn = 1 - 2 * d

    col = lax.broadcasted_iota(jnp.int32, (LANES, 2 * LANES), 0)
    ln2 = lax.broadcasted_iota(jnp.int32, (LANES, 2 * LANES), 1)
    want = (ln2 // LANES) * 16 + d * 8 + 2 * hp + (ln2 % LANES) // HEAD_DIM
    onehot = jnp.where(col == want, 1.0, 0.0).astype(BF16)
    gb = _dot_exact_rhs(gn, onehot)
    g_col = gb[:, :LANES]
    beta = gb[:, LANES:]
    pos = lax.broadcasted_iota(jnp.int32, (DN_TC, LANES), 0) % C
    gc = g_col
    shift = 1
    while shift < C:
        if d == 0:
            gc = gc + jnp.where(pos >= shift, pltpu.roll(gc, shift, 0), 0.0)
        else:
            gc = gc + jnp.where(pos < C - shift, pltpu.roll(gc, DN_TC - shift, 0), 0.0)
        shift *= 2
    last = C - 1 if d == 0 else 0
    tot = jnp.concatenate([jnp.broadcast_to(gc[C * j + last:C * j + last + 1, :], (C, LANES)) for j in range(nch)], axis=0)
    r2 = lax.broadcasted_iota(jnp.int32, (LANES, LANES), 0)
    c2 = lax.broadcasted_iota(jnp.int32, (LANES, LANES), 1)
    tri2 = jnp.where((r2 // C == c2 // C) & ((c2 - r2) * sgn >= 0), 1.0, 0.0).astype(BF16)
    gcr = _dot_exact_rhs(gr, tri2)

    egc = jnp.exp(gc)
    kbeta = k * beta
    vbeta = v * beta
    kbd = kbeta * egc
    qd = q * egc
    kd = k * jnp.exp(tot - gc)
    etot = jnp.exp(tot)

    row = lax.broadcasted_iota(jnp.int32, (C, LANES), 0)
    jj = lax.broadcasted_iota(jnp.int32, (C, LANES), 1) % HEAD_DIM
    earlier_eq = (row - jj) * sgn >= 0
    strict = jnp.where((row - jj) * sgn > 0, 1.0, 0.0)

    def chunks(a):
        return a.reshape(nch, C, a.shape[-1])

    gc3 = chunks(gc)
    diff = jnp.stack([gc3[j] - gcr[j:j + 1, :] for j in range(nch)])
    decay = jnp.where(earlier_eq, jnp.exp(jnp.minimum(diff, 0.0)), 0.0)
    return dict(q=chunks(q), k=chunks(k), kbeta=chunks(kbeta), vbeta=chunks(vbeta), kbd=chunks(kbd), qd=chunks(qd),
                kd=chunks(kd), etot=chunks(etot), decay=decay, decay_strict=decay * strict)


def _dn_solve(e):
    C = DN_CHUNK
    row = lax.broadcasted_iota(jnp.int32, (C, LANES), 0)
    lane = lax.broadcasted_iota(jnp.int32, (C, LANES), 1)
    lo = lane < HEAD_DIM
    eye2 = jnp.where(row == lane % HEAD_DIM, 1.0, 0.0)
    ai = jnp.einsum('bik,bjk->bij', jnp.concatenate([e['kbeta'], e['q']], axis=1).astype(BF16), _bd(e['k'], lo),
                    preferred_element_type=F32)
    intra = ai[:, C:] * e['decay']
    x = -ai[:, :C] * e['decay_strict']
    t = eye2 + x
    p = _bmm(x, _bd(x, lo))
    for lvl in range(5):
        pbd = _bd(p, lo)
        if lvl < 4:
            rr = _bmm(jnp.concatenate([p, t], axis=1), pbd)
            p = rr[:, :C]
            t = t + rr[:, C:]
        else:
            t = t + _bmm(t, pbd)
    uw = _bmm(t, jnp.concatenate([_bd(e['vbeta'], lo), _bd(e['kbd'], lo)], axis=2))
    kdt = jnp.stack([e['kd'][b].T for b in range(e['kd'].shape[0])]).astype(BF16)
    r3 = lax.broadcasted_iota(jnp.int32, (LANES, 2 * LANES), 0) // HEAD_DIM
    c3 = (lax.broadcasted_iota(jnp.int32, (LANES, 2 * LANES), 1) % LANES) // HEAD_DIM
    nm = jnp.where(r3 == c3, _bmm(kdt, uw.astype(BF16)), 0.0)
    return uw, intra, nm


def _dn_kernel(qf_ref, kf_ref, vf_ref, gnf_ref, grf_ref, qb_ref, kb_ref, vb_ref, gnb_ref, grb_ref,
               of_ref, ob_ref, s_ref):
    C = DN_CHUNK
    nch = DN_TC // C

    @pl.when(pl.program_id(2) == 0)
    def _():
        s_ref[...] = jnp.zeros_like(s_ref)

    in_refs = ((qf_ref, kf_ref, vf_ref, gnf_ref, grf_ref), (qb_ref, kb_ref, vb_ref, gnb_ref, grb_ref))
    out_refs = (of_ref, ob_ref)
    lo = lax.broadcasted_iota(jnp.int32, (C, LANES), 1) < HEAD_DIM

    chains = [(d, i) for d in range(2) for i in range(DN_NHP)]
    per = []
    for d, i in chains:
        q_ref, k_ref, v_ref, gn_ref, gr_ref = in_refs[d]
        sl = slice(LANES * i, LANES * (i + 1))
        hp = pl.program_id(1) * DN_NHP + i
        per.append(_dn_prepare(d, hp, q_ref[0, :, sl], k_ref[0, :, sl], v_ref[0, :, sl], gn_ref[0], gr_ref[0, i, 0]))
    e = {name: jnp.concatenate([p[name] for p in per], axis=0) for name in per[0]}
    uw, intra, nm = _dn_solve(e)
    qd, etot = e['qd'], e['etot']
    states = [s_ref[d, i] for d, i in chains]
    for step in range(nch):
        for n, (d, i) in enumerate(chains):
            j = step if d == 0 else nch - 1 - step
            b = n * nch + j
            s16 = states[n].astype(BF16)
            states[n] = (states[n] * etot[b, 0:1] + nm[b, :, :LANES]
                         - jnp.dot(nm[b, :, LANES:].astype(BF16), s16, preferred_element_type=F32))
            wq = jnp.dot(jnp.concatenate([uw[b, :, LANES:], qd[b]], axis=0).astype(BF16), s16,
                         preferred_element_type=F32)
            v_new = uw[b, :, :LANES] - wq[:C]
            out_refs[d][0, C * j:C * (j + 1), LANES * i:LANES * (i + 1)] = wq[C:] + jnp.dot(
                intra[b].astype(BF16), _bd(v_new, lo), preferred_element_type=F32)
    for n, (d, i) in enumerate(chains):
        s_ref[d, i] = states[n]


def _dn_call(dq, dk, dv, gn, gr):
    bsz, seq, _ = dq.shape
    nsc = seq // DN_TC
    nch = DN_TC // DN_CHUNK
    width = LANES * DN_NHP
    in_specs = []
    for d in range(2):
        tmap = (lambda c: c) if d == 0 else (lambda c: nsc - 1 - c)
        qkv_spec = pl.BlockSpec((1, DN_TC, width), lambda b, hp, c, tmap=tmap: (b, tmap(c), hp))
        in_specs += [
            qkv_spec, qkv_spec, qkv_spec,
            pl.BlockSpec((1, DN_TC, LANES), lambda b, hp, c, tmap=tmap: (b, tmap(c), 0)),
            pl.BlockSpec((1, DN_NHP, 1, nch, LANES), lambda b, hp, c, tmap=tmap, d=d: (d, hp, b, tmap(c), 0)),
        ]
    out_specs = [
        pl.BlockSpec((1, DN_TC, width), lambda b, hp, c: (b, c, hp)),
        pl.BlockSpec((1, DN_TC, width), lambda b, hp, c: (b, nsc - 1 - c, hp)),
    ]
    return pl.pallas_call(
        _dn_kernel,
        grid=(bsz, DN_HEADS // 2 // DN_NHP, nsc),
        in_specs=in_specs,
        out_specs=out_specs,
        out_shape=[jax.ShapeDtypeStruct((bsz, seq, 512), F32)] * 2,
        scratch_shapes=[pltpu.VMEM((2, DN_NHP, LANES, LANES), F32)],
        compiler_params=_cparams(("parallel", "parallel", "arbitrary")),
        name="deltanet",
    )(dq, dk, dv, gn, gr, dq, dk, dv, gn, gr)


def _dn_rows(gn, bsz, seq):
    g = gn[:, :32].reshape(bsz, seq // DN_CHUNK, DN_CHUNK, 2, 2, DN_HEADS // 2, 2)
    g = g[:, :, :, 0]
    g = jnp.transpose(g, (3, 4, 0, 1, 5, 2))
    return g.reshape(2, DN_HEADS // 2, bsz, seq // DN_CHUNK, LANES)


def _merge_kernel(x_ref, bra_ref, brb_ref, of_ref, ob_ref, cz_ref, wg_ref, bg_ref, wb_ref, wo_ref,
                  ng_ref, g1_ref, b1_ref, o_ref):
    x = x_ref[...]
    xb = x.astype(BF16)
    ones = _seg_ones(LANES, HEAD_DIM)
    parts = []
    for j in range(4):
        sl = slice(LANES * j, LANES * (j + 1))
        o = of_ref[:, sl] + ob_ref[:, sl]
        ss = _seg_sum(o * o, ones)
        parts.append((o * lax.rsqrt(ss * (1.0 / HEAD_DIM) + RMS_EPS) * ng_ref[...] * _silu(cz_ref[:, sl])).astype(BF16))
    brc = jnp.concatenate(parts, axis=1)
    merged = None
    for n, br in enumerate((bra_ref[...], brb_ref[...], brc)):
        sl = slice(D_MODEL * n, D_MODEL * (n + 1))
        gate = jax.nn.sigmoid(jnp.dot(xb, wg_ref[:, sl], preferred_element_type=F32) + bg_ref[:, sl])
        term = gate * jnp.dot(br, wb_ref[n], preferred_element_type=F32)
        merged = term if merged is None else merged + term
    y = DEEPNORM_ALPHA * x + jnp.dot(merged.astype(BF16), wo_ref[...], preferred_element_type=F32)
    o_ref[...] = _layer_norm(y, g1_ref[...], b1_ref[...])


def _merge_call(x, bra, brb, o_f, o_b, cz, wg, bg, wb, wo, ng, g1, b1, tm=512):
    t = x.shape[0]
    row = lambda i: (i, 0)
    return pl.pallas_call(
        _merge_kernel,
        grid=(t // tm,),
        in_specs=[
            pl.BlockSpec((tm, D_MODEL), row),
            pl.BlockSpec((tm, 512), row),
            pl.BlockSpec((tm, 512), row),
            pl.BlockSpec((tm, 512), row),
            pl.BlockSpec((tm, 512), row),
            pl.BlockSpec((tm, 512), row),
            _const_spec(wg.shape), _const_spec(bg.shape), _const_spec(wb.shape), _const_spec(wo.shape),
            _const_spec(ng.shape), _const_spec(g1.shape), _const_spec(b1.shape),
        ],
        out_specs=pl.BlockSpec((tm, D_MODEL), row),
        out_shape=jax.ShapeDtypeStruct((t, D_MODEL), F32),
        compiler_params=_cparams(("parallel",)),
        name="merge",
    )(x, bra, brb, o_f, o_b, cz, wg, bg, wb, wo, ng, g1, b1)


def _ffn_kernel(x_ref, pa_ref, pb_ref, wfg_ref, wfu_ref, wfo_ref, wpg_ref, bpg_ref, wpp_ref, g2_ref, b2_ref, o_ref,
                *, n_first):
    x = x_ref[...]
    xb = x.astype(BF16)
    p = jnp.where(pl.program_id(0) < n_first, pa_ref[0], pb_ref[0])
    gate = jnp.dot(xb, wfg_ref[...], preferred_element_type=F32)
    up = jnp.dot(xb, wfu_ref[...], preferred_element_type=F32)
    ffn = jnp.dot((_silu(gate) * up).astype(BF16), wfo_ref[...], preferred_element_type=F32)
    ple = (jax.nn.sigmoid(jnp.dot(xb, wpg_ref[...], preferred_element_type=F32) + bpg_ref[...])
           * jnp.dot(p.astype(BF16), wpp_ref[...], preferred_element_type=F32))
    o_ref[...] = _layer_norm(DEEPNORM_ALPHA * x + ffn + ple, g2_ref[...], b2_ref[...])


def _ffn_call(x, pa, pb, layer, wfg, wfu, wfo, wpg, bpg, wpp, g2, b2, tm=256):
    t = x.shape[0]
    n_first = pa.shape[1] // tm
    row = lambda i: (i, 0)
    return pl.pallas_call(
        functools.partial(_ffn_kernel, n_first=n_first),
        grid=(t // tm,),
        in_specs=[
            pl.BlockSpec((tm, D_MODEL), row),
            pl.BlockSpec((1, tm, PLE_DIM), lambda i: (layer, jnp.minimum(i, n_first - 1), 0)),
            pl.BlockSpec((1, tm, PLE_DIM), lambda i: (layer, jnp.maximum(i - n_first, 0), 0)),
            _const_spec(wfg.shape), _const_spec(wfu.shape), _const_spec(wfo.shape), _const_spec(wpg.shape),
            _const_spec(bpg.shape), _const_spec(wpp.shape), _const_spec(g2.shape), _const_spec(b2.shape),
        ],
        out_specs=pl.BlockSpec((tm, D_MODEL), row),
        out_shape=jax.ShapeDtypeStruct((t, D_MODEL), F32),
        compiler_params=_cparams(("arbitrary",)),
        name="ffn",
    )(x, pa, pb, wfg, wfu, wfo, wpg, bpg, wpp, g2, b2)


def _rope_tables(seq):
    t = jnp.arange(seq, dtype=jnp.int32)
    rowf = (t // GRID_W).astype(F32)
    colf = (t % GRID_W).astype(F32)
    inv_freq = ROPE_THETA ** (-jnp.arange(ROPE_PAIRS, dtype=F32) / ROPE_PAIRS)
    ang_r = rowf[:, None] * inv_freq[None, :]
    ang_c = colf[:, None] * inv_freq[None, :]
    cos = jnp.concatenate([jnp.cos(ang_r), jnp.cos(ang_r), jnp.cos(ang_c), jnp.cos(ang_c)], axis=1)
    sin = jnp.concatenate([-jnp.sin(ang_r), jnp.sin(ang_r), -jnp.sin(ang_c), jnp.sin(ang_c)], axis=1)
    return jnp.tile(cos, (1, 2)), jnp.tile(sin, (1, 2))


def _tile_lanes(v, reps):
    return jnp.tile(v.astype(F32), reps).reshape(1, -1)


def _layer(x, pa, pb, layer, bsz, seq, tables, w):
    (w_in, qn_g, kn_g, rpb, conv_w, a_log, dt_bias, dn_g, w_gate, b_gate, w_branch, w_out, ln1_g, ln1_b,
     w_ffn_in, w_ffn_out, w_ple_gate, b_ple_gate, w_ple_proj, ln2_g, ln2_b) = w
    o = _OFF
    wa = w_in[:, o[0]:o[3]].astype(BF16)
    wn = w_in[:, o[3]:o[6]].astype(BF16)
    wc = w_in[:, o[6]:o[9]].astype(BF16)
    wz = w_in[:, o[9]:o[10]].astype(BF16)
    wab = jnp.pad(w_in[:, o[10]:o[14]], ((0, 0), (0, LANES - 32))).astype(BF16)
    gp = jnp.zeros((8, LANES), F32)
    gp = gp.at[0, :16].set(a_log.reshape(16).astype(F32))
    gp = gp.at[1, :16].set(dt_bias.reshape(16).astype(F32))
    gp = gp.at[2, :16].set(1.0)
    cos_t, sin_t = tables
    (qt, ka, vt, nq, nkt, nv, dq, dk, dv, cz, gn) = _proj_call(
        x, bsz, seq, cos_t, sin_t, wa, wn, wc, wz, wab,
        _tile_lanes(qn_g, 2), _tile_lanes(kn_g, 2), conv_w.astype(F32), gp)
    r3 = lambda a: a.reshape(bsz, seq, a.shape[-1])
    bra = _gqa_call(qt, r3(ka), vt)
    brb = _na_call(r3(nq), nkt, r3(nv), rpb)
    o_f, o_b = _dn_call(r3(dq), r3(dk), r3(dv), r3(gn), _dn_rows(gn, bsz, seq))
    t = bsz * seq
    x = _merge_call(x, bra.reshape(t, 512), brb.reshape(t, 512), o_f.reshape(t, 512), o_b.reshape(t, 512), cz,
                    w_gate.astype(BF16), b_gate.astype(F32).reshape(1, -1), w_branch.astype(BF16),
                    w_out.astype(BF16), _tile_lanes(dn_g, 2), ln1_g.astype(F32).reshape(1, -1),
                    ln1_b.astype(F32).reshape(1, -1))
    x = _ffn_call(x, pa, pb, layer, w_ffn_in[:, :FFN_HIDDEN].astype(BF16), w_ffn_in[:, FFN_HIDDEN:].astype(BF16),
                  w_ffn_out.astype(BF16), w_ple_gate.astype(BF16), b_ple_gate.astype(F32).reshape(1, -1),
                  w_ple_proj.astype(BF16), ln2_g.astype(F32).reshape(1, -1), ln2_b.astype(F32).reshape(1, -1))
    return x


def _trunk(xa, xb, pa, pb, emb_ln_g, emb_ln_b, weights):
    (na, seq, d), nb = xa.shape, xb.shape[0]
    depth = pa.shape[0]
    bsz = na + nb
    xf = _ln_call(xa.reshape(na * seq, d), xb.reshape(nb * seq, d), emb_ln_g.astype(F32), emb_ln_b.astype(F32))
    pa = pa.reshape(depth, na * seq, -1)
    pb = pb.reshape(depth, nb * seq, -1)
    tables = _rope_tables(seq)
    for i in range(depth):
        xf = _layer(xf, pa, pb, i, bsz, seq, tables, tuple(a[i] for a in weights))
    y = xf.reshape(bsz, seq, d)
    return y[:na], y[na:]


def kernel(x_prompt, x_sample, p_prompt, p_sample, emb_ln_g, emb_ln_b, w_in, att_q_norm_g, att_k_norm_g, na_rpb, dn_conv_w, dn_a_log, dn_dt_bias, dn_norm_g, w_gate, b_gate, w_branch, w_out, ln1_g, ln1_b, w_ffn_in, w_ffn_out, w_ple_gate, b_ple_gate, w_ple_proj, ln2_g, ln2_b):
    weights = (w_in, att_q_norm_g, att_k_norm_g, na_rpb, dn_conv_w, dn_a_log, dn_dt_bias, dn_norm_g, w_gate,
               b_gate, w_branch, w_out, ln1_g, ln1_b, w_ffn_in, w_ffn_out, w_ple_gate, b_ple_gate, w_ple_proj,
               ln2_g, ln2_b)
    return _trunk(x_prompt, x_sample, p_prompt, p_sample, emb_ln_g, emb_ln_b, weights)
```
